```python
import math
import jax, jax.numpy as jnp
from jax import lax
import numpy as np

D_MODEL = 2048
BATCH = 4
SEQ = 4096
DEPTH = 4

MIX_WIDTH = D_MODEL
HEAD_DIM = 128
MLA_HEADS = 8
SB_HEADS = 8
MLA_WIDTH = MLA_HEADS * HEAD_DIM
SB_WIDTH = SB_HEADS * HEAD_DIM
Q_LORA_RANK = 768
KV_LORA_RANK = 512
QK_NOPE_DIM = 128
QK_ROPE_DIM = 64
V_HEAD_DIM = HEAD_DIM
MLA_QK_DIM = QK_NOPE_DIM + QK_ROPE_DIM
ROPE_THETA = 10000.0
D_FF = 5632
BLOCK_Q = 128
N_ADA = 9
DEEPNORM_ALPHA = (2.0 * DEPTH) ** 0.25
DEEPNORM_BETA = (8.0 * DEPTH) ** -0.25
FFN_RESIDUAL_WEIGHT = 0.5
LN_EPS = 1e-5
RMS_EPS = 1e-6
MASK_VALUE = -1e30
IN_COLS = Q_LORA_RANK + KV_LORA_RANK + QK_ROPE_DIM + 3 * SB_WIDTH

kernel_name = "hybrid_mla_stickbreaking_macaron_deepnorm_adaln"


def layer_norm(x, g, b):
    xf = x.astype(jnp.float32)
    mu = jnp.mean(xf, axis=-1, keepdims=True)
    var = jnp.mean(jnp.square(xf - mu), axis=-1, keepdims=True)
    y = (xf - mu) * lax.rsqrt(var + LN_EPS)
    return (y * g.astype(jnp.float32) + b.astype(jnp.float32)).astype(x.dtype)


def rms_norm(x, g):
    xf = x.astype(jnp.float32)
    y = xf * lax.rsqrt(jnp.mean(xf * xf, axis=-1, keepdims=True) + RMS_EPS)
    return (y * g.astype(jnp.float32)).astype(x.dtype)


def modulate(x, shift, scale):
    return x * (1.0 + scale[:, None, :]) + shift[:, None, :]


def swiglu_ffn(h, wi, wo):
    g, u = jnp.split(h @ wi, 2, axis=-1)
    return (jax.nn.silu(g) * u) @ wo


def rope(x, positions):
    half = QK_ROPE_DIM // 2
    inv_freq = ROPE_THETA ** (-jnp.arange(half, dtype=jnp.float32) / half)
    ang = positions.astype(jnp.float32)[..., None] * inv_freq
    cos = jnp.cos(ang)[:, :, None, :]
    sin = jnp.sin(ang)[:, :, None, :]
    xf = x.astype(jnp.float32)
    x1, x2 = xf[..., :half], xf[..., half:]
    return jnp.concatenate([x1 * cos - x2 * sin, x1 * sin + x2 * cos], axis=-1).astype(x.dtype)


def sweep_query_blocks(block_fn, q):
    B, S, H, D = q.shape
    nb = S // BLOCK_Q
    qb = q.reshape(B, nb, BLOCK_Q, H, D).transpose(1, 0, 2, 3, 4)
    out = lax.map(lambda args: block_fn(args[0], args[1]), (qb, jnp.arange(nb)))
    Dv = out.shape[-1]
    return out.transpose(1, 0, 2, 3, 4).reshape(B, S, H * Dv)


def causal_softmax_attention(q, k, v):
    S = k.shape[1]
    scale = q.shape[-1] ** -0.5
    k_idx = jnp.arange(S)

    def block(q_blk, blk):
        q_idx = blk * BLOCK_Q + jnp.arange(BLOCK_Q)
        s = jnp.einsum('bthd,bshd->bhts', q_blk, k).astype(jnp.float32) * scale
        mask = k_idx[None, :] <= q_idx[:, None]
        p = jax.nn.softmax(jnp.where(mask, s, MASK_VALUE), axis=-1).astype(v.dtype)
        return jnp.einsum('bhts,bshd->bthd', p, v)

    return sweep_query_blocks(block, q)


def stick_breaking_attention(q, k, v):
    S = k.shape[1]
    scale = q.shape[-1] ** -0.5
    k_idx = jnp.arange(S)

    def block(q_blk, blk):
        q_idx = blk * BLOCK_Q + jnp.arange(BLOCK_Q)
        z = jnp.einsum('bthd,bshd->bhts', q_blk, k).astype(jnp.float32) * scale
        mask = k_idx[None, :] < q_idx[:, None]
        log_beta = jax.nn.log_sigmoid(z)
        log_one_minus = jnp.where(mask, jax.nn.log_sigmoid(-z), 0.0)
        key_axis = log_one_minus.ndim - 1
        tail = lax.cumsum(log_one_minus, axis=key_axis, reverse=True) - log_one_minus
        a = jnp.where(mask, jnp.exp(log_beta + tail), 0.0).astype(v.dtype)
        return jnp.einsum('bhts,bshd->bthd', a, v)

    return sweep_query_blocks(block, q)


def hybrid_token_mixer(h, positions, w_in, q_norm_g, kv_norm_g, w_uq, w_ukv,
                       mla_out_g, sb_out_g, w_o):
    B, S, _ = h.shape
    proj = h @ w_in
    o1 = Q_LORA_RANK
    o2 = o1 + KV_LORA_RANK
    o3 = o2 + QK_ROPE_DIM
    c_q, c_kv, k_rope, sb_qkv = jnp.split(proj, [o1, o2, o3], axis=-1)

    q = (rms_norm(c_q, q_norm_g) @ w_uq).reshape(B, S, MLA_HEADS, MLA_QK_DIM)
    q_mla = jnp.concatenate([q[..., :QK_NOPE_DIM], rope(q[..., QK_NOPE_DIM:], positions)], axis=-1)
    kv = (rms_norm(c_kv, kv_norm_g) @ w_ukv).reshape(B, S, MLA_HEADS, QK_NOPE_DIM + V_HEAD_DIM)
    k_nope, v_mla = kv[..., :QK_NOPE_DIM], kv[..., QK_NOPE_DIM:]
    k_pe = rope(k_rope[:, :, None, :], positions)
    k_mla = jnp.concatenate(
        [k_nope, jnp.broadcast_to(k_pe, (B, S, MLA_HEADS, QK_ROPE_DIM))], axis=-1)
    o_mla = causal_softmax_attention(q_mla, k_mla, v_mla)

    sb_qkv = sb_qkv.reshape(B, S, 3, SB_HEADS, HEAD_DIM)
    o_sb = stick_breaking_attention(sb_qkv[:, :, 0], sb_qkv[:, :, 1], sb_qkv[:, :, 2])

    o = jnp.concatenate([rms_norm(o_mla, mla_out_g), rms_norm(o_sb, sb_out_g)], axis=-1)
    return o @ w_o


def setup_inputs(seed: int = 0) -> dict:
    key = jax.random.key(seed)
    ks = jax.random.split(key, 24)
    L, D, F = DEPTH, D_MODEL, D_FF

    def normal(k, shape, std):
        return jax.random.normal(k, shape, dtype=jnp.float32) * std

    x = normal(ks[0], (BATCH, SEQ, D), 1.0)
    c = normal(ks[1], (BATCH, D), 1.0)
    start = jax.random.randint(ks[2], (BATCH, 1), 0, 1024, dtype=jnp.int32)
    positions = (start + jnp.arange(SEQ, dtype=jnp.int32)[None, :]).astype(jnp.int32)

    ada_w = normal(ks[3], (L, D, N_ADA * D), 0.1 * D ** -0.5)
    ada_b = normal(ks[4], (L, N_ADA * D), 0.01)
    ln_g = 1.0 + normal(ks[5], (L, 3, D), 0.02)
    ln_b = normal(ks[6], (L, 3, D), 0.02)
    ffn1_wi = normal(ks[7], (L, D, 2 * F), D ** -0.5)
    ffn1_wo = normal(ks[8], (L, F, D), DEEPNORM_BETA * F ** -0.5)
    w_in = normal(ks[9], (L, D, IN_COLS), D ** -0.5)
    q_norm_g = 1.0 + normal(ks[10], (L, Q_LORA_RANK), 0.02)
    kv_norm_g = 1.0 + normal(ks[11], (L, KV_LORA_RANK), 0.02)
    w_uq = normal(ks[12], (L, Q_LORA_RANK, MLA_HEADS * MLA_QK_DIM), Q_LORA_RANK ** -0.5)
    w_ukv = normal(ks[13], (L, KV_LORA_RANK, MLA_HEADS * (QK_NOPE_DIM + V_HEAD_DIM)), KV_LORA_RANK ** -0.5)
    mla_out_g = 1.0 + normal(ks[14], (L, MLA_WIDTH), 0.02)
    sb_out_g = 1.0 + normal(ks[15], (L, SB_WIDTH), 0.02)
    w_o = normal(ks[16], (L, MIX_WIDTH, D), DEEPNORM_BETA * MIX_WIDTH ** -0.5)
    ffn2_wi = normal(ks[17], (L, D, 2 * F), D ** -0.5)
    ffn2_wo = normal(ks[18], (L, F, D), DEEPNORM_BETA * F ** -0.5)
    return {"x": x, "c": c, "positions": positions,
            "ada_w": ada_w, "ada_b": ada_b, "ln_g": ln_g, "ln_b": ln_b,
            "ffn1_wi": ffn1_wi, "ffn1_wo": ffn1_wo,
            "w_in": w_in, "q_norm_g": q_norm_g, "kv_norm_g": kv_norm_g,
            "w_uq": w_uq, "w_ukv": w_ukv, "mla_out_g": mla_out_g, "sb_out_g": sb_out_g,
            "w_o": w_o, "ffn2_wi": ffn2_wi, "ffn2_wo": ffn2_wo}


def reference(x, c, positions, ada_w, ada_b, ln_g, ln_b, ffn1_wi, ffn1_wo,
              w_in, q_norm_g, kv_norm_g, w_uq, w_ukv, mla_out_g, sb_out_g,
              w_o, ffn2_wi, ffn2_wo):
    c_act = jax.nn.silu(c)
    for l in range(DEPTH):
        ada = c_act @ ada_w[l] + ada_b[l]
        sh1, sc1, g1, shm, scm, gm, sh2, sc2, g2 = jnp.split(ada, N_ADA, axis=-1)

        f = swiglu_ffn(modulate(x, sh1, sc1), ffn1_wi[l], ffn1_wo[l])
        x = layer_norm(DEEPNORM_ALPHA * x + FFN_RESIDUAL_WEIGHT * (1.0 + g1)[:, None, :] * f,
                       ln_g[l, 0], ln_b[l, 0])

        m = hybrid_token_mixer(modulate(x, shm, scm), positions, w_in[l], q_norm_g[l], kv_norm_g[l],
                               w_uq[l], w_ukv[l], mla_out_g[l], sb_out_g[l], w_o[l])
        x = layer_norm(DEEPNORM_ALPHA * x + (1.0 + gm)[:, None, :] * m, ln_g[l, 1], ln_b[l, 1])

        f = swiglu_ffn(modulate(x, sh2, sc2), ffn2_wi[l], ffn2_wo[l])
        x = layer_norm(DEEPNORM_ALPHA * x + FFN_RESIDUAL_WEIGHT * (1.0 + g2)[:, None, :] * f,
                       ln_g[l, 2], ln_b[l, 2])
    return x
```

```python
import functools

import jax
import jax.numpy as jnp
from jax import lax
from jax.experimental import pallas as pl
from jax.experimental.pallas import tpu as pltpu

HEAD_DIM = 128
MLA_HEADS = 8
SB_HEADS = 8
Q_LORA_RANK = 768
KV_LORA_RANK = 512
QK_NOPE_DIM = 128
QK_ROPE_DIM = 64
MLA_QK_DIM = QK_NOPE_DIM + QK_ROPE_DIM
ROPE_THETA = 10000.0
N_ADA = 9
LN_EPS = 1e-5
RMS_EPS = 1e-6
FFN_RESIDUAL_WEIGHT = 0.5

LANES = 128
MLA_QK_PAD = 2 * LANES
VMEM_LIMIT_BYTES = 56 * 1024 * 1024

F32 = jnp.float32
BF16 = jnp.bfloat16


def _params(*semantics):
    return pltpu.CompilerParams(dimension_semantics=semantics,
                                vmem_limit_bytes=VMEM_LIMIT_BYTES)


def _resident(shape):
    zeros = (0,) * len(shape)
    return pl.BlockSpec(shape, lambda *_: zeros, pipeline_mode=pl.Buffered(1))


def _layer_norm(y, g, b):
    mu = jnp.mean(y, axis=-1, keepdims=True)
    d = y - mu
    var = jnp.mean(d * d, axis=-1, keepdims=True)
    return d * lax.rsqrt(var + LN_EPS) * g + b


def _rms_norm(y, g):
    return y * lax.rsqrt(jnp.mean(y * y, axis=-1, keepdims=True) + RMS_EPS) * g


def _ada_kernel(c_ref, w_ref, b_ref, o_ref):
    c = c_ref[...]
    c_act = (c * jax.nn.sigmoid(c)).astype(BF16)
    acc = jnp.dot(c_act, w_ref[...].astype(BF16), preferred_element_type=F32)
    o_ref[...] = acc + b_ref[...]


def _ada_all_layers(c, ada_w, ada_b, tn=1024):
    L, D, N = ada_w.shape
    B = c.shape[0]
    tn = min(tn, D)
    assert D % tn == 0
    return pl.pallas_call(
        _ada_kernel,
        grid=(L, N // tn),
        in_specs=[
            pl.BlockSpec((B, D), lambda l, j: (0, 0)),
            pl.BlockSpec((None, D, tn), lambda l, j: (l, 0, j)),
            pl.BlockSpec((None, 1, tn), lambda l, j: (l, 0, j)),
        ],
        out_specs=pl.BlockSpec((None, B, tn), lambda l, j: (l, 0, j)),
        out_shape=jax.ShapeDtypeStruct((L, B, N), F32),
        compiler_params=_params("arbitrary", "arbitrary"),
        name="ada",
    )(c, ada_w, ada_b.reshape(L, 1, N))


def _rope_table_kernel(pos_ref, invf_ref, cos_ref, sinlo_ref, sinhi_ref):
    half = QK_ROPE_DIM // 2
    ang = pos_ref[...].astype(F32) * invf_ref[...]
    lane = lax.broadcasted_iota(jnp.int32, ang.shape, 1)
    cos = jnp.cos(ang)
    sin = jnp.sin(ang)
    cos_ref[...] = jnp.where(lane < QK_ROPE_DIM, cos, 0.0)
    sinlo_ref[...] = jnp.where(lane < half, -sin, 0.0)
    sinhi_ref[...] = jnp.where((lane >= half) & (lane < QK_ROPE_DIM), sin, 0.0)


def _rope_tables(positions, tm=1024):
    M = positions.size
    half = QK_ROPE_DIM // 2
    inv_freq = ROPE_THETA ** (-jnp.arange(half, dtype=F32) / half)
    invf = jnp.tile(inv_freq, LANES // half).reshape(1, LANES)
    tm = min(tm, M)
    spec = pl.BlockSpec((tm, LANES), lambda i: (i, 0))
    shape = jax.ShapeDtypeStruct((M, LANES), F32)
    return pl.pallas_call(
        _rope_table_kernel,
        grid=(M // tm,),
        in_specs=[pl.BlockSpec((tm, 1), lambda i: (i, 0)),
                  pl.BlockSpec((1, LANES), lambda i: (0, 0))],
        out_specs=[spec, spec, spec],
        out_shape=[shape, shape, shape],
        compiler_params=_params("arbitrary"),
        name="rope_tables",
    )(positions.reshape(M, 1), invf)


def _ffn_kernel(x_ref, sh_ref, sc_ref, gt_ref, wg_ref, wu_ref, wo_ref, lng_ref, lnb_ref,
                o_ref, h_ref, *, alpha):
    j = pl.program_id(1)

    @pl.when(j == 0)
    def _():
        h_ref[...] = (x_ref[...] * (1.0 + sc_ref[...]) + sh_ref[...]).astype(BF16)
        o_ref[...] = jnp.zeros_like(o_ref)

    h = h_ref[...]
    g = jnp.dot(h, wg_ref[...], preferred_element_type=F32)
    u = jnp.dot(h, wu_ref[...], preferred_element_type=F32)
    a = (g * jax.nn.sigmoid(g) * u).astype(BF16)
    o_ref[...] += jnp.dot(a, wo_ref[...], preferred_element_type=F32)

    @pl.when(j == pl.num_programs(1) - 1)
    def _():
        y = alpha * x_ref[...] + (FFN_RESIDUAL_WEIGHT * (1.0 + gt_ref[...])) * o_ref[...]
        o_ref[...] = _layer_norm(y, lng_ref[...], lnb_ref[...])


def _ffn_block(x, ada, k_shift, wi, wo, ln_g, ln_b, *, alpha, rows_per_batch, tm=1024, tf=512):
    M, D = x.shape
    F = wo.shape[0]
    tm = min(tm, rows_per_batch)
    tf = min(tf, F)
    nf = F // tf
    tiles_per_batch = rows_per_batch // tm

    def ada_spec(k):
        return pl.BlockSpec((None, None, 1, D), lambda i, j: (i // tiles_per_batch, k, 0, 0))

    vec = pl.BlockSpec((1, D), lambda i, j: (0, 0))
    return pl.pallas_call(
        functools.partial(_ffn_kernel, alpha=alpha),
        grid=(M // tm, nf),
        in_specs=[
            pl.BlockSpec((tm, D), lambda i, j: (i, 0), pipeline_mode=pl.Buffered(1)),
            ada_spec(k_shift), ada_spec(k_shift + 1), ada_spec(k_shift + 2),
            pl.BlockSpec((D, tf), lambda i, j: (0, j)),
            pl.BlockSpec((D, tf), lambda i, j: (0, j + nf)),
            pl.BlockSpec((tf, D), lambda i, j: (j, 0)),
            vec, vec,
        ],
        out_specs=pl.BlockSpec((tm, D), lambda i, j: (i, 0)),
        out_shape=jax.ShapeDtypeStruct((M, D), F32),
        scratch_shapes=[pltpu.VMEM((tm, D), BF16)],
        compiler_params=_params("arbitrary", "arbitrary"),
        name="ffn",
    )(x, ada, ada, ada, wi, wi, wo, ln_g.reshape(1, D), ln_b.reshape(1, D))


def _inproj_kernel(x_ref, sh_ref, sc_ref, cos_ref, sinlo_ref, sinhi_ref,
                   wcq_ref, wckv_ref, wkr_ref, wsb_ref, qng_ref, kvng_ref,
                   wuqn_ref, wuqr_ref, wuk_ref, wuv_ref,
                   q_ref, k_ref, v_ref, sq_ref, sk_ref, sv_ref):
    h = (x_ref[...] * (1.0 + sc_ref[...]) + sh_ref[...]).astype(BF16)
    cos, sin_lo, sin_hi = cos_ref[...], sinlo_ref[...], sinhi_ref[...]
    half = QK_ROPE_DIM // 2

    def rope(xr):
        return (xr * cos + pltpu.roll(xr, LANES - half, axis=1) * sin_lo
                + pltpu.roll(xr, half, axis=1) * sin_hi)

    sb_w = SB_HEADS * HEAD_DIM
    sb = jnp.dot(h, wsb_ref[...], preferred_element_type=F32)
    sq_ref[...] = (sb[:, :sb_w] * (HEAD_DIM ** -0.5)).astype(BF16)
    sk_ref[...] = sb[:, sb_w:2 * sb_w].astype(BF16)
    sv_ref[...] = sb[:, 2 * sb_w:].astype(BF16)

    c_q = jnp.dot(h, wcq_ref[...], preferred_element_type=F32)
    c_qn = _rms_norm(c_q, qng_ref[...]).astype(BF16)
    q_nope = jnp.dot(c_qn, wuqn_ref[...], preferred_element_type=F32)
    q_rope = jnp.dot(c_qn, wuqr_ref[...], preferred_element_type=F32)
    q_scale = MLA_QK_DIM ** -0.5
    for hd in range(MLA_HEADS):
        lo = hd * MLA_QK_PAD
        q_ref[:, lo:lo + LANES] = (q_nope[:, hd * LANES:(hd + 1) * LANES] * q_scale).astype(BF16)
        q_ref[:, lo + LANES:lo + 2 * LANES] = (
            rope(q_rope[:, hd * LANES:(hd + 1) * LANES]) * q_scale).astype(BF16)

    c_kv = jnp.dot(h, wckv_ref[...], preferred_element_type=F32)
    c_kvn = _rms_norm(c_kv, kvng_ref[...]).astype(BF16)
    k_nope = jnp.dot(c_kvn, wuk_ref[...], preferred_element_type=F32)
    v_ref[...] = jnp.dot(c_kvn, wuv_ref[...], preferred_element_type=F32).astype(BF16)
    k_pe = rope(jnp.dot(h, wkr_ref[...], preferred_element_type=F32)).astype(BF16)
    for hd in range(MLA_HEADS):
        lo = hd * MLA_QK_PAD
        k_ref[:, lo:lo + LANES] = k_nope[:, hd * LANES:(hd + 1) * LANES].astype(BF16)
        k_ref[:, lo + LANES:lo + 2 * LANES] = k_pe


def _inproj_block(x, ada, k_shift, tables, w, *, rows_per_batch, tm=256):
    M, D = x.shape
    tm = min(tm, rows_per_batch)
    tiles_per_batch = rows_per_batch // tm

    def ada_spec(k):
        return pl.BlockSpec((None, None, 1, D), lambda i: (i // tiles_per_batch, k, 0, 0))

    def rows(width):
        return pl.BlockSpec((tm, width), lambda i: (i, 0))

    weights = [w["cq"], w["ckv"], w["kr"], w["sb"], w["qn_g"], w["kvn_g"],
               w["uq_nope"], w["uq_rope"], w["uk"], w["uv"]]
    widths = [MLA_HEADS * MLA_QK_PAD, MLA_HEADS * MLA_QK_PAD, MLA_HEADS * HEAD_DIM,
              SB_HEADS * HEAD_DIM, SB_HEADS * HEAD_DIM, SB_HEADS * HEAD_DIM]
    return pl.pallas_call(
        _inproj_kernel,
        grid=(M // tm,),
        in_specs=[rows(D), ada_spec(k_shift), ada_spec(k_shift + 1),
                  rows(LANES), rows(LANES), rows(LANES)]
                 + [_resident(a.shape) for a in weights],
        out_specs=[rows(n) for n in widths],
        out_shape=[jax.ShapeDtypeStruct((M, n), BF16) for n in widths],
        compiler_params=_params("arbitrary"),
        name="inproj",
    )(x, ada, ada, *tables, *weights)


def _mla_attn_kernel(q_ref, k_ref, v_ref, o_ref, *, tk):
    qi = pl.program_id(2)
    q = q_ref[...]
    tq = q.shape[0]

    def scores(c):
        k = k_ref[pl.ds(pl.multiple_of(c * tk, tk), tk), :]
        return lax.dot_general(q, k, (((1,), (1,)), ((), ())), preferred_element_type=F32)

    def update(c, s, carry):
        m, l, acc = carry
        m_new = jnp.maximum(m, jnp.max(s, axis=-1, keepdims=True))
        corr = jnp.exp(m - m_new)
        p = jnp.exp(s - m_new)
        v = v_ref[pl.ds(pl.multiple_of(c * tk, tk), tk), :]
        acc = acc * corr + jnp.dot(p.astype(BF16), v, preferred_element_type=F32)
        return m_new, l * corr + jnp.sum(p, axis=-1, keepdims=True), acc

    row = lax.broadcasted_iota(jnp.int32, (tq, tk), 0)
    col = lax.broadcasted_iota(jnp.int32, (tq, tk), 1)
    s = jnp.where(col <= row, scores(qi), -jnp.inf)
    init = (jnp.full((tq, 1), -jnp.inf, F32), jnp.zeros((tq, 1), F32),
            jnp.zeros((tq, v_ref.shape[-1]), F32))
    carry = update(qi, s, init)
    m, l, acc = lax.fori_loop(0, qi, lambda c, carry: update(c, scores(c), carry), carry)
    o_ref[...] = acc / l


def _mla_attention(q, k, v, *, batch, seq, t=512):
    M = q.shape[0]
    t = min(t, seq)
    nq = seq // t
    k3 = k.reshape(batch, seq, k.shape[1])
    v3 = v.reshape(batch, seq, v.shape[1])
    return pl.pallas_call(
        functools.partial(_mla_attn_kernel, tk=t),
        grid=(batch, MLA_HEADS, nq),
        in_specs=[
            pl.BlockSpec((t, MLA_QK_PAD), lambda b, h, i: (b * nq + i, h)),
            pl.BlockSpec((None, seq, MLA_QK_PAD), lambda b, h, i: (b, 0, h)),
            pl.BlockSpec((None, seq, HEAD_DIM), lambda b, h, i: (b, 0, h)),
        ],
        out_specs=pl.BlockSpec((t, HEAD_DIM), lambda b, h, i: (b * nq + i, h)),
        out_shape=jax.ShapeDtypeStruct((M, MLA_HEADS * HEAD_DIM), F32),
        compiler_params=_params("arbitrary", "arbitrary", "arbitrary"),
        name="mla_attn",
    )(q, k3, v3)


def _sb_attn_kernel(q_ref, k_ref, v_ref, o_ref, *, tk):
    qi = pl.program_id(2)
    q = q_ref[...]
    tq = q.shape[0]
    jj = lax.broadcasted_iota(jnp.int32, (tk, tk), 0)
    ss = lax.broadcasted_iota(jnp.int32, (tk, tk), 1)
    suffix = (jj > ss).astype(BF16)

    def block(c, carry, mask):
        run, acc = carry
        start = pl.multiple_of(c * tk, tk)
        k = k_ref[pl.ds(start, tk), :]
        z = lax.dot_general(q, k, (((1,), (1,)), ((), ())), preferred_element_type=F32)
        log_beta = jnp.minimum(z, 0.0) - jnp.log1p(jnp.exp(-jnp.abs(z)))
        log_om = log_beta - z
        if mask is not None:
            log_om = jnp.where(mask, log_om, 0.0)
        hi = log_om.astype(BF16)
        lo = (log_om - hi.astype(F32)).astype(BF16)
        tail = (jnp.dot(hi, suffix, preferred_element_type=F32)
                + jnp.dot(lo, suffix, preferred_element_type=F32)) + run
        a = jnp.exp(log_beta + tail)
        if mask is not None:
            a = jnp.where(mask, a, 0.0)
        v = v_ref[pl.ds(start, tk), :]
        acc = acc + jnp.dot(a.astype(BF16), v, preferred_element_type=F32)
        return run + jnp.sum(log_om, axis=-1, keepdims=True), acc

    row = lax.broadcasted_iota(jnp.int32, (tq, tk), 0)
    col = lax.broadcasted_iota(jnp.int32, (tq, tk), 1)
    init = (jnp.zeros((tq, 1), F32), jnp.zeros((tq, v_ref.shape[-1]), F32))
    carry = block(qi, init, col < row)
    _, acc = lax.fori_loop(0, qi, lambda n, carry: block(qi - 1 - n, carry, None), carry)
    o_ref[...] = acc


def _sb_attention(q, k, v, *, batch, seq, t=512):
    M = q.shape[0]
    t = min(t, seq)
    nq = seq // t
    k3 = k.reshape(batch, seq, k.shape[1])
    v3 = v.reshape(batch, seq, v.shape[1])
    return pl.pallas_call(
        functools.partial(_sb_attn_kernel, tk=t),
        grid=(batch, SB_HEADS, nq),
        in_specs=[
            pl.BlockSpec((t, HEAD_DIM), lambda b, h, i: (b * nq + i, h)),
            pl.BlockSpec((None, seq, HEAD_DIM), lambda b, h, i: (b, 0, h)),
            pl.BlockSpec((None, seq, HEAD_DIM), lambda b, h, i: (b, 0, h)),
        ],
        out_specs=pl.BlockSpec((t, HEAD_DIM), lambda b, h, i: (b * nq + i, h)),
        out_shape=jax.ShapeDtypeStruct((M, SB_HEADS * HEAD_DIM), F32),
        compiler_params=_params("arbitrary", "arbitrary", "arbitrary"),
        name="sb_attn",
    )(q, k3, v3)


def _outproj_kernel(x_ref, om_ref, os_ref, gt_ref, mg_ref, sg_ref, wom_ref, wos_ref,
                    lng_ref, lnb_ref, o_ref, *, alpha):
    om = _rms_norm(om_ref[...], mg_ref[...]).astype(BF16)
    os_ = _rms_norm(os_ref[...], sg_ref[...]).astype(BF16)
    m = (jnp.dot(om, wom_ref[...], preferred_element_type=F32)
         + jnp.dot(os_, wos_ref[...], preferred_element_type=F32))
    y = alpha * x_ref[...] + (1.0 + gt_ref[...]) * m
    o_ref[...] = _layer_norm(y, lng_ref[...], lnb_ref[...])


def _outproj_block(x, o_mla, o_sb, ada, k_gate, w, ln_g, ln_b, *, alpha, rows_per_batch, tm=512):
    M, D = x.shape
    tm = min(tm, rows_per_batch)
    tiles_per_batch = rows_per_batch // tm

    def rows(width):
        return pl.BlockSpec((tm, width), lambda i: (i, 0))

    consts = [w["mla_out_g"], w["sb_out_g"], w["o_mla"], w["o_sb"],
              ln_g.reshape(1, D), ln_b.reshape(1, D)]
    return pl.pallas_call(
        functools.partial(_outproj_kernel, alpha=alpha),
        grid=(M // tm,),
        in_specs=[rows(D), rows(o_mla.shape[1]), rows(o_sb.shape[1]),
                  pl.BlockSpec((None, None, 1, D), lambda i: (i // tiles_per_batch, k_gate, 0, 0))]
                 + [_resident(a.shape) for a in consts],
        out_specs=rows(D),
        out_shape=jax.ShapeDtypeStruct((M, D), F32),
        compiler_params=_params("arbitrary"),
        name="outproj",
    )(x, o_mla, o_sb, ada, *consts)


def _mixer_weights(w_in, q_norm_g, kv_norm_g, w_uq, w_ukv, mla_out_g, sb_out_g, w_o):
    o1 = Q_LORA_RANK
    o2 = o1 + KV_LORA_RANK
    o3 = o2 + QK_ROPE_DIM
    w_in = w_in.astype(BF16)
    uq = w_uq.astype(BF16).reshape(Q_LORA_RANK, MLA_HEADS, MLA_QK_DIM)
    uq_rope = jnp.pad(uq[:, :, QK_NOPE_DIM:], ((0, 0), (0, 0), (0, LANES - QK_ROPE_DIM)))
    ukv = w_ukv.astype(BF16).reshape(KV_LORA_RANK, MLA_HEADS, QK_NOPE_DIM + HEAD_DIM)
    w_o = w_o.astype(BF16)
    mla_w = MLA_HEADS * HEAD_DIM
    return {
        "cq": w_in[:, :o1],
        "ckv": w_in[:, o1:o2],
        "kr": jnp.pad(w_in[:, o2:o3], ((0, 0), (0, LANES - QK_ROPE_DIM))),
        "sb": w_in[:, o3:],
        "qn_g": q_norm_g.reshape(1, -1),
        "kvn_g": kv_norm_g.reshape(1, -1),
        "uq_nope": uq[:, :, :QK_NOPE_DIM].reshape(Q_LORA_RANK, MLA_HEADS * QK_NOPE_DIM),
        "uq_rope": uq_rope.reshape(Q_LORA_RANK, MLA_HEADS * LANES),
        "uk": ukv[:, :, :QK_NOPE_DIM].reshape(KV_LORA_RANK, MLA_HEADS * QK_NOPE_DIM),
        "uv": ukv[:, :, QK_NOPE_DIM:].reshape(KV_LORA_RANK, MLA_HEADS * HEAD_DIM),
        "mla_out_g": mla_out_g.reshape(1, -1),
        "sb_out_g": sb_out_g.reshape(1, -1),
        "o_mla": w_o[:mla_w],
        "o_sb": w_o[mla_w:],
    }


def kernel(x, c, positions, ada_w, ada_b, ln_g, ln_b, ffn1_wi, ffn1_wo, w_in, q_norm_g, kv_norm_g, w_uq, w_ukv, mla_out_g, sb_out_g, w_o, ffn2_wi, ffn2_wo):
    B, S, D = x.shape
    depth = ada_w.shape[0]
    alpha = (2.0 * depth) ** 0.25
    M = B * S

    ada_all = _ada_all_layers(c, ada_w, ada_b).reshape(depth, B, N_ADA, 1, D)
    tables = _rope_tables(positions)
    xf = x.reshape(M, D)
    for l in range(depth):
        ada = ada_all[l]
        xf = _ffn_block(xf, ada, 0, ffn1_wi[l].astype(BF16), ffn1_wo[l].astype(BF16),
                        ln_g[l, 0], ln_b[l, 0], alpha=alpha, rows_per_batch=S)
        w = _mixer_weights(w_in[l], q_norm_g[l], kv_norm_g[l], w_uq[l], w_ukv[l],
                           mla_out_g[l], sb_out_g[l], w_o[l])
        q, k, v, sq, sk, sv = _inproj_block(xf, ada, 3, tables, w, rows_per_batch=S)
        o_mla = _mla_attention(q, k, v, batch=B, seq=S)
        o_sb = _sb_attention(sq, sk, sv, batch=B, seq=S)
        xf = _outproj_block(xf, o_mla, o_sb, ada, 5, w, ln_g[l, 1], ln_b[l, 1],
                            alpha=alpha, rows_per_batch=S)
        xf = _ffn_block(xf, ada, 6, ffn2_wi[l].astype(BF16), ffn2_wo[l].astype(BF16),
                        ln_g[l, 2], ln_b[l, 2], alpha=alpha, rows_per_batch=S)
    return xf.reshape(B, S, D)
```

```python
import functools

import jax
import jax.numpy as jnp
from jax import lax
from jax.experimental import pallas as pl
from jax.experimental.pallas import tpu as pltpu

HEAD_DIM = 128
MLA_HEADS = 8
SB_HEADS = 8
Q_LORA_RANK = 768
KV_LORA_RANK = 512
QK_NOPE_DIM = 128
QK_ROPE_DIM = 64
MLA_QK_DIM = QK_NOPE_DIM + QK_ROPE_DIM
ROPE_THETA = 10000.0
N_ADA = 9
LN_EPS = 1e-5
RMS_EPS = 1e-6
FFN_RESIDUAL_WEIGHT = 0.5
LOG2_E = 1.4426950408889634

LANES = 128
MLA_QK_PAD = 2 * LANES
VMEM_LIMIT_BYTES = 56 * 1024 * 1024

F32 = jnp.float32
BF16 = jnp.bfloat16


def _params(*semantics):
    return pltpu.CompilerParams(dimension_semantics=semantics,
                                vmem_limit_bytes=VMEM_LIMIT_BYTES)


def _resident(shape):
    zeros = (0,) * len(shape)
    return pl.BlockSpec(shape, lambda *_: zeros, pipeline_mode=pl.Buffered(1))


def _layer_norm(y, g, b):
    mu = jnp.mean(y, axis=-1, keepdims=True)
    d = y - mu
    var = jnp.mean(d * d, axis=-1, keepdims=True)
    return d * lax.rsqrt(var + LN_EPS) * g + b


def _rms_norm(y, g):
    return y * lax.rsqrt(jnp.mean(y * y, axis=-1, keepdims=True) + RMS_EPS) * g


def _ada_kernel(c_ref, w_ref, b_ref, o_ref):
    c = c_ref[...]
    c_act = (c * jax.nn.sigmoid(c)).astype(BF16)
    acc = jnp.dot(c_act, w_ref[...].astype(BF16), preferred_element_type=F32)
    o_ref[...] = acc + b_ref[...]


def _ada_all_layers(c, ada_w, ada_b, tn=1024):
    L, D, N = ada_w.shape
    B = c.shape[0]
    tn = min(tn, D)
    assert D % tn == 0
    return pl.pallas_call(
        _ada_kernel,
        grid=(L, N // tn),
        in_specs=[
            pl.BlockSpec((B, D), lambda l, j: (0, 0)),
            pl.BlockSpec((None, D, tn), lambda l, j: (l, 0, j)),
            pl.BlockSpec((None, 1, tn), lambda l, j: (l, 0, j)),
        ],
        out_specs=pl.BlockSpec((None, B, tn), lambda l, j: (l, 0, j)),
        out_shape=jax.ShapeDtypeStruct((L, B, N), F32),
        compiler_params=_params("arbitrary", "arbitrary"),
        name="ada",
    )(c, ada_w, ada_b.reshape(L, 1, N))


def _rope_table_kernel(pos_ref, invf_ref, cos_ref, sinlo_ref, sinhi_ref):
    half = QK_ROPE_DIM // 2
    ang = pos_ref[...].astype(F32) * invf_ref[...]
    lane = lax.broadcasted_iota(jnp.int32, ang.shape, 1)
    cos = jnp.cos(ang)
    sin = jnp.sin(ang)
    cos_ref[...] = jnp.where(lane < QK_ROPE_DIM, cos, 0.0)
    sinlo_ref[...] = jnp.where(lane < half, -sin, 0.0)
    sinhi_ref[...] = jnp.where((lane >= half) & (lane < QK_ROPE_DIM), sin, 0.0)


def _rope_tables(positions, tm=1024):
    M = positions.size
    half = QK_ROPE_DIM // 2
    inv_freq = ROPE_THETA ** (-jnp.arange(half, dtype=F32) / half)
    invf = jnp.tile(inv_freq, LANES // half).reshape(1, LANES)
    tm = min(tm, M)
    spec = pl.BlockSpec((tm, LANES), lambda i: (i, 0))
    shape = jax.ShapeDtypeStruct((M, LANES), F32)
    return pl.pallas_call(
        _rope_table_kernel,
        grid=(M // tm,),
        in_specs=[pl.BlockSpec((tm, 1), lambda i: (i, 0)),
                  pl.BlockSpec((1, LANES), lambda i: (0, 0))],
        out_specs=[spec, spec, spec],
        out_shape=[shape, shape, shape],
        compiler_params=_params("arbitrary"),
        name="rope_tables",
    )(positions.reshape(M, 1), invf)


def _ffn_kernel(x_ref, sh_ref, sc_ref, gt_ref, wg_ref, wu_ref, wo_ref, lng_ref, lnb_ref,
                o_ref, h_ref, *, alpha):
    j = pl.program_id(1)

    @pl.when(j == 0)
    def _():
        h_ref[...] = (x_ref[...] * (1.0 + sc_ref[...]) + sh_ref[...]).astype(BF16)
        o_ref[...] = jnp.zeros_like(o_ref)

    h = h_ref[...]
    g = jnp.dot(h, wg_ref[...], preferred_element_type=F32)
    u = jnp.dot(h, wu_ref[...], preferred_element_type=F32)
    a = (g * jax.nn.sigmoid(g) * u).astype(BF16)
    o_ref[...] += jnp.dot(a, wo_ref[...], preferred_element_type=F32)

    @pl.when(j == pl.num_programs(1) - 1)
    def _():
        y = alpha * x_ref[...] + (FFN_RESIDUAL_WEIGHT * (1.0 + gt_ref[...])) * o_ref[...]
        o_ref[...] = _layer_norm(y, lng_ref[...], lnb_ref[...])


def _ffn_block(x, ada, k_shift, wi, wo, ln_g, ln_b, *, alpha, rows_per_batch, tm=1024, tf=512):
    M, D = x.shape
    F = wo.shape[0]
    tm = min(tm, rows_per_batch)
    tf = min(tf, F)
    nf = F // tf
    tiles_per_batch = rows_per_batch // tm

    def ada_spec(k):
        return pl.BlockSpec((None, None, 1, D), lambda i, j: (i // tiles_per_batch, k, 0, 0))

    vec = pl.BlockSpec((1, D), lambda i, j: (0, 0))
    return pl.pallas_call(
        functools.partial(_ffn_kernel, alpha=alpha),
        grid=(M // tm, nf),
        in_specs=[
            pl.BlockSpec((tm, D), lambda i, j: (i, 0), pipeline_mode=pl.Buffered(1)),
            ada_spec(k_shift), ada_spec(k_shift + 1), ada_spec(k_shift + 2),
            pl.BlockSpec((D, tf), lambda i, j: (0, j)),
            pl.BlockSpec((D, tf), lambda i, j: (0, j + nf)),
            pl.BlockSpec((tf, D), lambda i, j: (j, 0)),
            vec, vec,
        ],
        out_specs=pl.BlockSpec((tm, D), lambda i, j: (i, 0)),
        out_shape=jax.ShapeDtypeStruct((M, D), F32),
        scratch_shapes=[pltpu.VMEM((tm, D), BF16)],
        compiler_params=_params("arbitrary", "arbitrary"),
        name="ffn",
    )(x, ada, ada, ada, wi, wi, wo, ln_g.reshape(1, D), ln_b.reshape(1, D))


def _inproj_kernel(x_ref, sh_ref, sc_ref, cos_ref, sinlo_ref, sinhi_ref,
                   wcq_ref, wckv_ref, wkr_ref, wsb_ref, qng_ref, kvng_ref,
                   wuqn_ref, wuqr_ref, wuk_ref, wuv_ref,
                   q_ref, k_ref, v_ref, sq_ref, sk_ref, sv_ref):
    h = (x_ref[...] * (1.0 + sc_ref[...]) + sh_ref[...]).astype(BF16)
    cos, sin_lo, sin_hi = cos_ref[...], sinlo_ref[...], sinhi_ref[...]
    half = QK_ROPE_DIM // 2

    def rope(xr):
        return (xr * cos + pltpu.roll(xr, LANES - half, axis=1) * sin_lo
                + pltpu.roll(xr, half, axis=1) * sin_hi)

    sb_w = SB_HEADS * HEAD_DIM
    sb = jnp.dot(h, wsb_ref[...], preferred_element_type=F32)
    sq_ref[...] = (sb[:, :sb_w] * (LOG2_E * HEAD_DIM ** -0.5)).astype(BF16)
    sk_ref[...] = sb[:, sb_w:2 * sb_w].astype(BF16)
    sv_ref[...] = sb[:, 2 * sb_w:].astype(BF16)

    c_q = jnp.dot(h, wcq_ref[...], preferred_element_type=F32)
    c_qn = _rms_norm(c_q, qng_ref[...]).astype(BF16)
    q_nope = jnp.dot(c_qn, wuqn_ref[...], preferred_element_type=F32)
    q_rope = jnp.dot(c_qn, wuqr_ref[...], preferred_element_type=F32)
    q_scale = LOG2_E * MLA_QK_DIM ** -0.5
    for hd in range(MLA_HEADS):
        lo = hd * MLA_QK_PAD
        q_ref[:, lo:lo + LANES] = (q_nope[:, hd * LANES:(hd + 1) * LANES] * q_scale).astype(BF16)
        q_ref[:, lo + LANES:lo + 2 * LANES] = (
            rope(q_rope[:, hd * LANES:(hd + 1) * LANES]) * q_scale).astype(BF16)

    c_kv = jnp.dot(h, wckv_ref[...], preferred_element_type=F32)
    c_kvn = _rms_norm(c_kv, kvng_ref[...]).astype(BF16)
    k_nope = jnp.dot(c_kvn, wuk_ref[...], preferred_element_type=F32)
    v_ref[...] = jnp.dot(c_kvn, wuv_ref[...], preferred_element_type=F32).astype(BF16)
    k_pe = rope(jnp.dot(h, wkr_ref[...], preferred_element_type=F32)).astype(BF16)
    for hd in range(MLA_HEADS):
        lo = hd * MLA_QK_PAD
        k_ref[:, lo:lo + LANES] = k_nope[:, hd * LANES:(hd + 1) * LANES].astype(BF16)
        k_ref[:, lo + LANES:lo + 2 * LANES] = k_pe


def _inproj_block(x, ada, k_shift, tables, w, *, rows_per_batch, tm=256):
    M, D = x.shape
    tm = min(tm, rows_per_batch)
    tiles_per_batch = rows_per_batch // tm

    def ada_spec(k):
        return pl.BlockSpec((None, None, 1, D), lambda i: (i // tiles_per_batch, k, 0, 0))

    def rows(width):
        return pl.BlockSpec((tm, width), lambda i: (i, 0))

    weights = [w["cq"], w["ckv"], w["kr"], w["sb"], w["qn_g"], w["kvn_g"],
               w["uq_nope"], w["uq_rope"], w["uk"], w["uv"]]
    widths = [MLA_HEADS * MLA_QK_PAD, MLA_HEADS * MLA_QK_PAD, MLA_HEADS * HEAD_DIM,
              SB_HEADS * HEAD_DIM, SB_HEADS * HEAD_DIM, SB_HEADS * HEAD_DIM]
    return pl.pallas_call(
        _inproj_kernel,
        grid=(M // tm,),
        in_specs=[rows(D), ada_spec(k_shift), ada_spec(k_shift + 1),
                  rows(LANES), rows(LANES), rows(LANES)]
                 + [_resident(a.shape) for a in weights],
        out_specs=[rows(n) for n in widths],
        out_shape=[jax.ShapeDtypeStruct((M, n), BF16) for n in widths],
        compiler_params=_params("arbitrary"),
        name="inproj",
    )(x, ada, ada, *tables, *weights)


def _mla_attn_kernel(q_ref, k_ref, v_ref, o_ref, *, tk):
    qi = pl.program_id(2)
    tq = q_ref.shape[0]
    heads = q_ref.shape[1] // MLA_QK_PAD

    def step(c, carries, mask):
        kv_rows = pl.ds(pl.multiple_of(c * tk, tk), tk)
        scores = [lax.dot_general(q_ref[:, hd * MLA_QK_PAD:(hd + 1) * MLA_QK_PAD],
                                  k_ref[kv_rows, hd * MLA_QK_PAD:(hd + 1) * MLA_QK_PAD],
                                  (((1,), (1,)), ((), ())), preferred_element_type=F32)
                  for hd in range(heads)]
        stats, probs = [], []
        for s, (m, l, _) in zip(scores, carries):
            if mask is not None:
                s = jnp.where(mask, s, -jnp.inf)
            m_new = jnp.maximum(m, jnp.max(s, axis=-1, keepdims=True))
            corr = jnp.exp2(m - m_new)
            p = jnp.exp2(s - m_new)
            stats.append((m_new, l * corr + jnp.sum(p, axis=-1, keepdims=True), corr))
            probs.append(p.astype(BF16))
        out = []
        for hd, (p, (m_new, l_new, corr), (_, _, acc)) in enumerate(zip(probs, stats, carries)):
            pv = jnp.dot(p, v_ref[kv_rows, hd * HEAD_DIM:(hd + 1) * HEAD_DIM],
                         preferred_element_type=F32)
            out.append((m_new, l_new, acc * corr + pv))
        return tuple(out)

    row = lax.broadcasted_iota(jnp.int32, (tq, tk), 0)
    col = lax.broadcasted_iota(jnp.int32, (tq, tk), 1)
    init = tuple((jnp.full((tq, 1), -jnp.inf, F32), jnp.zeros((tq, 1), F32),
                  jnp.zeros((tq, HEAD_DIM), F32)) for _ in range(heads))
    carries = step(qi, init, col <= row)
    carries = lax.fori_loop(0, qi, lambda c, cs: step(c, cs, None), carries)
    for hd, (_, l, acc) in enumerate(carries):
        o_ref[:, hd * HEAD_DIM:(hd + 1) * HEAD_DIM] = acc / l


def _mla_attention(q, k, v, *, batch, seq, t=512, heads_per_step=2):
    M = q.shape[0]
    t = min(t, seq)
    nq = seq // t
    k3 = k.reshape(batch, seq, k.shape[1])
    v3 = v.reshape(batch, seq, v.shape[1])
    qk_width = heads_per_step * MLA_QK_PAD
    v_width = heads_per_step * HEAD_DIM
    return pl.pallas_call(
        functools.partial(_mla_attn_kernel, tk=t),
        grid=(batch, MLA_HEADS // heads_per_step, nq),
        in_specs=[
            pl.BlockSpec((t, qk_width), lambda b, h, i: (b * nq + i, h)),
            pl.BlockSpec((None, seq, qk_width), lambda b, h, i: (b, 0, h)),
            pl.BlockSpec((None, seq, v_width), lambda b, h, i: (b, 0, h)),
        ],
        out_specs=pl.BlockSpec((t, v_width), lambda b, h, i: (b * nq + i, h)),
        out_shape=jax.ShapeDtypeStruct((M, MLA_HEADS * HEAD_DIM), F32),
        compiler_params=_params("arbitrary", "arbitrary", "arbitrary"),
        name="mla_attn",
    )(q, k3, v3)


def _sb_attn_kernel(q_ref, k_ref, v_ref, o_ref, *, tk, rq):
    qi = pl.program_id(2)
    tq = q_ref.shape[0]
    heads = q_ref.shape[1] // HEAD_DIM
    blocks_per_step = tq // tk
    chunks = [(hd, r) for hd in range(heads) for r in range(tq // rq)]
    jj = lax.broadcasted_iota(jnp.int32, (tk, tk), 0)
    ss = lax.broadcasted_iota(jnp.int32, (tk, tk), 1)
    suffix = (jj > ss).astype(BF16)
    sign_bit = jnp.int32(-2 ** 31)

    def step(first, carries, masked):
        row = lax.broadcasted_iota(jnp.int32, (rq, tk), 0)
        col = lax.broadcasted_iota(jnp.int32, (rq, tk), 1)
        tiles = []
        for d in reversed(range(blocks_per_step)):
            for u, (_, r) in enumerate(chunks):
                if masked and d * tk >= (r + 1) * rq - 1:
                    continue
                partial = masked and (d + 1) * tk - 1 >= r * rq
                tiles.append((u, d, (col + d * tk < row + r * rq) if partial else None))

        def kv_rows(d):
            return pl.ds(pl.multiple_of((first + d) * tk, tk), tk)

        def cols(u):
            hd = chunks[u][0]
            return slice(hd * HEAD_DIM, (hd + 1) * HEAD_DIM)

        runs = [run for run, _ in carries]
        accs = [acc for _, acc in carries]
        n = len(tiles)
        zs, log_betas, tails = [None] * n, [None] * n, [None] * n

        def scores(i):
            u, d, _ = tiles[i]
            r = chunks[u][1]
            zs[i] = lax.dot_general(q_ref[r * rq:(r + 1) * rq, cols(u)],
                                    k_ref[kv_rows(d), cols(u)],
                                    (((1,), (1,)), ((), ())), preferred_element_type=F32)

        def log_terms(i):
            u, d, mask = tiles[i]
            z = zs[i]
            neg_abs = pltpu.bitcast(pltpu.bitcast(z, jnp.int32) | sign_bit, F32)
            softplus = jnp.log(1.0 + jnp.exp2(neg_abs)) * LOG2_E
            log_beta = jnp.minimum(z, 0.0) - softplus
            log_om = log_beta - z
            if mask is not None:
                log_om = jnp.where(mask, log_om, 0.0)
            log_betas[i] = log_beta + runs[u]
            runs[u] = runs[u] + jnp.sum(log_om, axis=-1, keepdims=True)
            tails[i] = jnp.dot(log_om.astype(BF16), suffix, preferred_element_type=F32)

        def accumulate(i):
            u, d, mask = tiles[i]
            a = jnp.exp2(log_betas[i] + tails[i])
            if mask is not None:
                a = jnp.where(mask, a, 0.0)
            accs[u] = accs[u] + jnp.dot(a.astype(BF16), v_ref[kv_rows(d), cols(u)],
                                        preferred_element_type=F32)

        group, lag = 2, 1
        groups = [range(s, min(s + group, n)) for s in range(0, n, group)]
        for t in range(len(groups) + 1 + lag):
            for stage, back in ((scores, 0), (log_terms, 1), (accumulate, 1 + lag)):
                if 0 <= t - back < len(groups):
                    for i in groups[t - back]:
                        stage(i)
        return tuple(zip(runs, accs))

    init = tuple((jnp.zeros((rq, 1), F32), jnp.zeros((rq, HEAD_DIM), F32)) for _ in chunks)
    carries = step(qi * blocks_per_step, init, True)
    carries = lax.fori_loop(
        0, qi, lambda n, cs: step((qi - 1 - n) * blocks_per_step, cs, False), carries)
    for (hd, r), (_, acc) in zip(chunks, carries):
        o_ref[r * rq:(r + 1) * rq, hd * HEAD_DIM:(hd + 1) * HEAD_DIM] = acc


def _sb_attention(q, k, v, *, batch, seq, tq=512, tk=256, rq=128, heads_per_step=2):
    M = q.shape[0]
    t = min(tq, seq)
    tk = min(tk, t)
    rq = min(rq, t)
    nq = seq // t
    width = heads_per_step * HEAD_DIM
    k3 = k.reshape(batch, seq, k.shape[1])
    v3 = v.reshape(batch, seq, v.shape[1])
    return pl.pallas_call(
        functools.partial(_sb_attn_kernel, tk=tk, rq=rq),
        grid=(batch, SB_HEADS // heads_per_step, nq),
        in_specs=[
            pl.BlockSpec((t, width), lambda b, h, i: (b * nq + i, h)),
            pl.BlockSpec((None, seq, width), lambda b, h, i: (b, 0, h)),
            pl.BlockSpec((None, seq, width), lambda b, h, i: (b, 0, h)),
        ],
        out_specs=pl.BlockSpec((t, width), lambda b, h, i: (b * nq + i, h)),
        out_shape=jax.ShapeDtypeStruct((M, SB_HEADS * HEAD_DIM), F32),
        compiler_params=_params("arbitrary", "arbitrary", "arbitrary"),
        name="sb_attn",
    )(q, k3, v3)


def _outproj_kernel(x_ref, om_ref, os_ref, gt_ref, mg_ref, sg_ref, wom_ref, wos_ref,
                    lng_ref, lnb_ref, o_ref, *, alpha):
    om = _rms_norm(om_ref[...], mg_ref[...]).astype(BF16)
    os_ = _rms_norm(os_ref[...], sg_ref[...]).astype(BF16)
    m = (jnp.dot(om, wom_ref[...], preferred_element_type=F32)
         + jnp.dot(os_, wos_ref[...], preferred_element_type=F32))
    y = alpha * x_ref[...] + (1.0 + gt_ref[...]) * m
    o_ref[...] = _layer_norm(y, lng_ref[...], lnb_ref[...])


def _outproj_block(x, o_mla, o_sb, ada, k_gate, w, ln_g, ln_b, *, alpha, rows_per_batch, tm=512):
    M, D = x.shape
    tm = min(tm, rows_per_batch)
    tiles_per_batch = rows_per_batch // tm

    def rows(width):
        return pl.BlockSpec((tm, width), lambda i: (i, 0))

    consts = [w["mla_out_g"], w["sb_out_g"], w["o_mla"], w["o_sb"],
              ln_g.reshape(1, D), ln_b.reshape(1, D)]
    return pl.pallas_call(
        functools.partial(_outproj_kernel, alpha=alpha),
        grid=(M // tm,),
        in_specs=[rows(D), rows(o_mla.shape[1]), rows(o_sb.shape[1]),
                  pl.BlockSpec((None, None, 1, D), lambda i: (i // tiles_per_batch, k_gate, 0, 0))]
                 + [_resident(a.shape) for a in consts],
        out_specs=rows(D),
        out_shape=jax.ShapeDtypeStruct((M, D), F32),
        compiler_params=_params("arbitrary"),
        name="outproj",
    )(x, o_mla, o_sb, ada, *consts)


def _mixer_weights(w_in, q_norm_g, kv_norm_g, w_uq, w_ukv, mla_out_g, sb_out_g, w_o):
    o1 = Q_LORA_RANK
    o2 = o1 + KV_LORA_RANK
    o3 = o2 + QK_ROPE_DIM
    w_in = w_in.astype(BF16)
    uq = w_uq.astype(BF16).reshape(Q_LORA_RANK, MLA_HEADS, MLA_QK_DIM)
    uq_rope = jnp.pad(uq[:, :, QK_NOPE_DIM:], ((0, 0), (0, 0), (0, LANES - QK_ROPE_DIM)))
    ukv = w_ukv.astype(BF16).reshape(KV_LORA_RANK, MLA_HEADS, QK_NOPE_DIM + HEAD_DIM)
    w_o = w_o.astype(BF16)
    mla_w = MLA_HEADS * HEAD_DIM
    return {
        "cq": w_in[:, :o1],
        "ckv": w_in[:, o1:o2],
        "kr": jnp.pad(w_in[:, o2:o3], ((0, 0), (0, LANES - QK_ROPE_DIM))),
        "sb": w_in[:, o3:],
        "qn_g": q_norm_g.reshape(1, -1),
        "kvn_g": kv_norm_g.reshape(1, -1),
        "uq_nope": uq[:, :, :QK_NOPE_DIM].reshape(Q_LORA_RANK, MLA_HEADS * QK_NOPE_DIM),
        "uq_rope": uq_rope.reshape(Q_LORA_RANK, MLA_HEADS * LANES),
        "uk": ukv[:, :, :QK_NOPE_DIM].reshape(KV_LORA_RANK, MLA_HEADS * QK_NOPE_DIM),
        "uv": ukv[:, :, QK_NOPE_DIM:].reshape(KV_LORA_RANK, MLA_HEADS * HEAD_DIM),
        "mla_out_g": mla_out_g.reshape(1, -1),
        "sb_out_g": sb_out_g.reshape(1, -1),
        "o_mla": w_o[:mla_w],
        "o_sb": w_o[mla_w:],
    }


def kernel(x, c, positions, ada_w, ada_b, ln_g, ln_b, ffn1_wi, ffn1_wo, w_in, q_norm_g, kv_norm_g, w_uq, w_ukv, mla_out_g, sb_out_g, w_o, ffn2_wi, ffn2_wo):
    B, S, D = x.shape
    depth = ada_w.shape[0]
    alpha = (2.0 * depth) ** 0.25
    M = B * S

    ada_all = _ada_all_layers(c, ada_w, ada_b).reshape(depth, B, N_ADA, 1, D)
    tables = _rope_tables(positions)
    xf = x.reshape(M, D)
    for l in range(depth):
        ada = ada_all[l]
        xf = _ffn_block(xf, ada, 0, ffn1_wi[l].astype(BF16), ffn1_wo[l].astype(BF16),
                        ln_g[l, 0], ln_b[l, 0], alpha=alpha, rows_per_batch=S)
        w = _mixer_weights(w_in[l], q_norm_g[l], kv_norm_g[l], w_uq[l], w_ukv[l],
                           mla_out_g[l], sb_out_g[l], w_o[l])
        q, k, v, sq, sk, sv = _inproj_block(xf, ada, 3, tables, w, rows_per_batch=S)
        o_mla = _mla_attention(q, k, v, batch=B, seq=S)
        o_sb = _sb_attention(sq, sk, sv, batch=B, seq=S)
        xf = _outproj_block(xf, o_mla, o_sb, ada, 5, w, ln_g[l, 1], ln_b[l, 1],
                            alpha=alpha, rows_per_batch=S)
        xf = _ffn_block(xf, ada, 6, ffn2_wi[l].astype(BF16), ffn2_wo[l].astype(BF16),
                        ln_g[l, 2], ln_b[l, 2], alpha=alpha, rows_per_batch=S)
    return xf.reshape(B, S, D)
```

```python
import functools

import jax
import jax.numpy as jnp
from jax import lax
from jax.experimental import pallas as pl
from jax.experimental.pallas import tpu as pltpu

HEAD_DIM = 128
MLA_HEADS = 8
SB_HEADS = 8
Q_LORA_RANK = 768
KV_LORA_RANK = 512
QK_NOPE_DIM = 128
QK_ROPE_DIM = 64
MLA_QK_DIM = QK_NOPE_DIM + QK_ROPE_DIM
ROPE_THETA = 10000.0
N_ADA = 9
LN_EPS = 1e-5
RMS_EPS = 1e-6
FFN_RESIDUAL_WEIGHT = 0.5
LOG2_E = 1.4426950408889634
SB_DEAD_LOG2 = -152.0

LANES = 128
MLA_QK_PAD = 2 * LANES
VMEM_LIMIT_BYTES = 56 * 1024 * 1024

F32 = jnp.float32
BF16 = jnp.bfloat16


def _params(*semantics):
    return pltpu.CompilerParams(dimension_semantics=semantics,
                                vmem_limit_bytes=VMEM_LIMIT_BYTES)


def _layer_resident(stacked, l):
    index = (l,) + (0,) * (stacked.ndim - 1)
    return pl.BlockSpec((None,) + stacked.shape[1:], lambda *_: index,
                        pipeline_mode=pl.Buffered(1))


def _layer_norm(y, g, b):
    mu = jnp.mean(y, axis=-1, keepdims=True)
    d = y - mu
    var = jnp.mean(d * d, axis=-1, keepdims=True)
    return d * lax.rsqrt(var + LN_EPS) * g + b


def _rms_norm(y, g):
    return y * lax.rsqrt(jnp.mean(y * y, axis=-1, keepdims=True) + RMS_EPS) * g


def _ada_kernel(c_ref, w_ref, b_ref, o_ref):
    c = c_ref[...]
    c_act = (c * jax.nn.sigmoid(c)).astype(BF16)
    acc = jnp.dot(c_act, w_ref[...].astype(BF16), preferred_element_type=F32)
    o_ref[...] = acc + b_ref[...]


def _ada_all_layers(c, ada_w, ada_b, tn=1024):
    L, D, N = ada_w.shape
    B = c.shape[0]
    tn = min(tn, D)
    assert D % tn == 0
    return pl.pallas_call(
        _ada_kernel,
        grid=(L, N // tn),
        in_specs=[
            pl.BlockSpec((B, D), lambda l, j: (0, 0)),
            pl.BlockSpec((None, D, tn), lambda l, j: (l, 0, j)),
            pl.BlockSpec((None, 1, tn), lambda l, j: (l, 0, j)),
        ],
        out_specs=pl.BlockSpec((None, B, tn), lambda l, j: (l, 0, j)),
        out_shape=jax.ShapeDtypeStruct((L, B, N), F32),
        compiler_params=_params("arbitrary", "arbitrary"),
        name="ada",
    )(c, ada_w, ada_b.reshape(L, 1, N))


def _rope_table_kernel(pos_ref, invf_ref, cos_ref, sinlo_ref, sinhi_ref):
    half = QK_ROPE_DIM // 2
    ang = pos_ref[...].astype(F32) * invf_ref[...]
    lane = lax.broadcasted_iota(jnp.int32, ang.shape, 1)
    cos = jnp.cos(ang)
    sin = jnp.sin(ang)
    cos_ref[...] = jnp.where(lane < QK_ROPE_DIM, cos, 0.0)
    sinlo_ref[...] = jnp.where(lane < half, -sin, 0.0)
    sinhi_ref[...] = jnp.where((lane >= half) & (lane < QK_ROPE_DIM), sin, 0.0)


def _rope_tables(positions, tm=1024):
    M = positions.size
    half = QK_ROPE_DIM // 2
    inv_freq = ROPE_THETA ** (-jnp.arange(half, dtype=F32) / half)
    invf = jnp.tile(inv_freq, LANES // half).reshape(1, LANES)
    tm = min(tm, M)
    spec = pl.BlockSpec((tm, LANES), lambda i: (i, 0))
    shape = jax.ShapeDtypeStruct((M, LANES), F32)
    return pl.pallas_call(
        _rope_table_kernel,
        grid=(M // tm,),
        in_specs=[pl.BlockSpec((tm, 1), lambda i: (i, 0)),
                  pl.BlockSpec((1, LANES), lambda i: (0, 0))],
        out_specs=[spec, spec, spec],
        out_shape=[shape, shape, shape],
        compiler_params=_params("arbitrary"),
        name="rope_tables",
    )(positions.reshape(M, 1), invf)


def _ffn_kernel(x_ref, sh_ref, sc_ref, gt_ref, wg_ref, wu_ref, wo_ref, lng_ref, lnb_ref,
                o_ref, h_ref, *, alpha):
    j = pl.program_id(1)

    @pl.when(j == 0)
    def _():
        h_ref[...] = (x_ref[...] * (1.0 + sc_ref[...]) + sh_ref[...]).astype(BF16)
        o_ref[...] = jnp.zeros_like(o_ref)

    h = h_ref[...]
    g = jnp.dot(h, wg_ref[...], preferred_element_type=F32)
    u = jnp.dot(h, wu_ref[...], preferred_element_type=F32)
    a = (g * jax.nn.sigmoid(g) * u).astype(BF16)
    o_ref[...] += jnp.dot(a, wo_ref[...], preferred_element_type=F32)

    @pl.when(j == pl.num_programs(1) - 1)
    def _():
        y = alpha * x_ref[...] + (FFN_RESIDUAL_WEIGHT * (1.0 + gt_ref[...])) * o_ref[...]
        o_ref[...] = _layer_norm(y, lng_ref[...], lnb_ref[...])


def _ffn_block(x, ada, l, k_shift, wi, wo, ln_g, ln_b, k_ln, *, alpha, rows_per_batch,
               tm=1024, tf=512):
    M, D = x.shape
    F = wo.shape[1]
    tm = min(tm, rows_per_batch)
    tf = min(tf, F)
    nf = F // tf
    tiles_per_batch = rows_per_batch // tm

    def ada_spec(k):
        return pl.BlockSpec((None, None, None, 1, D),
                            lambda i, j: (l, i // tiles_per_batch, k, 0, 0))

    ln_spec = pl.BlockSpec((None, None, 1, D), lambda i, j: (l, k_ln, 0, 0))
    return pl.pallas_call(
        functools.partial(_ffn_kernel, alpha=alpha),
        grid=(M // tm, nf),
        in_specs=[
            pl.BlockSpec((tm, D), lambda i, j: (i, 0), pipeline_mode=pl.Buffered(1)),
            ada_spec(k_shift), ada_spec(k_shift + 1), ada_spec(k_shift + 2),
            pl.BlockSpec((None, D, tf), lambda i, j: (l, 0, j)),
            pl.BlockSpec((None, D, tf), lambda i, j: (l, 0, j + nf)),
            pl.BlockSpec((None, tf, D), lambda i, j: (l, j, 0)),
            ln_spec, ln_spec,
        ],
        out_specs=pl.BlockSpec((tm, D), lambda i, j: (i, 0)),
        out_shape=jax.ShapeDtypeStruct((M, D), F32),
        scratch_shapes=[pltpu.VMEM((tm, D), BF16)],
        compiler_params=_params("arbitrary", "arbitrary"),
        name="ffn",
    )(x, ada, ada, ada, wi, wi, wo, ln_g, ln_b)


def _inproj_kernel(x_ref, sh_ref, sc_ref, cos_ref, sinlo_ref, sinhi_ref,
                   wcq_ref, wckv_ref, wkr_ref, wsb_ref, qng_ref, kvng_ref,
                   wuqn_ref, wuqr_ref, wuk_ref, wuv_ref,
                   q_ref, k_ref, v_ref, sq_ref, sk_ref, sv_ref):
    h = (x_ref[...] * (1.0 + sc_ref[...]) + sh_ref[...]).astype(BF16)
    cos, sin_lo, sin_hi = cos_ref[...], sinlo_ref[...], sinhi_ref[...]
    half = QK_ROPE_DIM // 2

    def rope(xr):
        return (xr * cos + pltpu.roll(xr, LANES - half, axis=1) * sin_lo
                + pltpu.roll(xr, half, axis=1) * sin_hi)

    sb_w = SB_HEADS * HEAD_DIM
    sb = jnp.dot(h, wsb_ref[...], preferred_element_type=F32)
    sq_ref[...] = (sb[:, :sb_w] * (LOG2_E * HEAD_DIM ** -0.5)).astype(BF16)
    sk_ref[...] = sb[:, sb_w:2 * sb_w].astype(BF16)
    sv_ref[...] = sb[:, 2 * sb_w:].astype(BF16)

    c_q = jnp.dot(h, wcq_ref[...], preferred_element_type=F32)
    c_qn = _rms_norm(c_q, qng_ref[...]).astype(BF16)
    q_nope = jnp.dot(c_qn, wuqn_ref[...], preferred_element_type=F32)
    q_rope = jnp.dot(c_qn, wuqr_ref[...], preferred_element_type=F32)
    q_scale = LOG2_E * MLA_QK_DIM ** -0.5
    for hd in range(MLA_HEADS):
        lo = hd * MLA_QK_PAD
        q_ref[:, lo:lo + LANES] = (q_nope[:, hd * LANES:(hd + 1) * LANES] * q_scale).astype(BF16)
        q_ref[:, lo + LANES:lo + 2 * LANES] = (
            rope(q_rope[:, hd * LANES:(hd + 1) * LANES]) * q_scale).astype(BF16)

    c_kv = jnp.dot(h, wckv_ref[...], preferred_element_type=F32)
    c_kvn = _rms_norm(c_kv, kvng_ref[...]).astype(BF16)
    k_nope = jnp.dot(c_kvn, wuk_ref[...], preferred_element_type=F32)
    v_ref[...] = jnp.dot(c_kvn, wuv_ref[...], preferred_element_type=F32).astype(BF16)
    k_pe = rope(jnp.dot(h, wkr_ref[...], preferred_element_type=F32)).astype(BF16)
    for hd in range(MLA_HEADS):
        lo = hd * MLA_QK_PAD
        k_ref[:, lo:lo + LANES] = k_nope[:, hd * LANES:(hd + 1) * LANES].astype(BF16)
        k_ref[:, lo + LANES:lo + 2 * LANES] = k_pe


def _inproj_block(x, ada, l, k_shift, tables, w, *, rows_per_batch, tm=256):
    M, D = x.shape
    tm = min(tm, rows_per_batch)
    tiles_per_batch = rows_per_batch // tm

    def ada_spec(k):
        return pl.BlockSpec((None, None, None, 1, D),
                            lambda i: (l, i // tiles_per_batch, k, 0, 0))

    def rows(width):
        return pl.BlockSpec((tm, width), lambda i: (i, 0))

    weights = [w["cq"], w["ckv"], w["kr"], w["sb"], w["qn_g"], w["kvn_g"],
               w["uq_nope"], w["uq_rope"], w["uk"], w["uv"]]
    widths = [MLA_HEADS * MLA_QK_PAD, MLA_HEADS * MLA_QK_PAD, MLA_HEADS * HEAD_DIM,
              SB_HEADS * HEAD_DIM, SB_HEADS * HEAD_DIM, SB_HEADS * HEAD_DIM]
    return pl.pallas_call(
        _inproj_kernel,
        grid=(M // tm,),
        in_specs=[rows(D), ada_spec(k_shift), ada_spec(k_shift + 1),
                  rows(LANES), rows(LANES), rows(LANES)]
                 + [_layer_resident(a, l) for a in weights],
        out_specs=[rows(n) for n in widths],
        out_shape=[jax.ShapeDtypeStruct((M, n), BF16) for n in widths],
        compiler_params=_params("arbitrary"),
        name="inproj",
    )(x, ada, ada, *tables, *weights)


def _mla_attn_kernel(q_ref, k_ref, v_ref, o_ref, *, tk):
    qi = pl.program_id(2)
    tq = q_ref.shape[0]
    heads = q_ref.shape[1] // MLA_QK_PAD

    def step(c, carries, mask):
        kv_rows = pl.ds(pl.multiple_of(c * tk, tk), tk)
        scores = [lax.dot_general(q_ref[:, hd * MLA_QK_PAD:(hd + 1) * MLA_QK_PAD],
                                  k_ref[kv_rows, hd * MLA_QK_PAD:(hd + 1) * MLA_QK_PAD],
                                  (((1,), (1,)), ((), ())), preferred_element_type=F32)
                  for hd in range(heads)]
        stats, probs = [], []
        for s, (m, l, _) in zip(scores, carries):
            if mask is not None:
                s = jnp.where(mask, s, -jnp.inf)
            m_new = jnp.maximum(m, jnp.max(s, axis=-1, keepdims=True))
            corr = jnp.exp2(m - m_new)
            p = jnp.exp2(s - m_new)
            stats.append((m_new, l * corr + jnp.sum(p, axis=-1, keepdims=True), corr))
            probs.append(p.astype(BF16))
        out = []
        for hd, (p, (m_new, l_new, corr), (_, _, acc)) in enumerate(zip(probs, stats, carries)):
            pv = jnp.dot(p, v_ref[kv_rows, hd * HEAD_DIM:(hd + 1) * HEAD_DIM],
                         preferred_element_type=F32)
            out.append((m_new, l_new, acc * corr + pv))
        return tuple(out)

    row = lax.broadcasted_iota(jnp.int32, (tq, tk), 0)
    col = lax.broadcasted_iota(jnp.int32, (tq, tk), 1)
    init = tuple((jnp.full((tq, 1), -jnp.inf, F32), jnp.zeros((tq, 1), F32),
                  jnp.zeros((tq, HEAD_DIM), F32)) for _ in range(heads))
    carries = step(qi, init, col <= row)
    carries = lax.fori_loop(0, qi, lambda c, cs: step(c, cs, None), carries)
    for hd, (_, l, acc) in enumerate(carries):
        o_ref[:, hd * HEAD_DIM:(hd + 1) * HEAD_DIM] = acc / l


def _mla_attention(q, k, v, *, batch, seq, t=512, heads_per_step=2):
    M = q.shape[0]
    t = min(t, seq)
    nq = seq // t
    k3 = k.reshape(batch, seq, k.shape[1])
    v3 = v.reshape(batch, seq, v.shape[1])
    qk_width = heads_per_step * MLA_QK_PAD
    v_width = heads_per_step * HEAD_DIM
    return pl.pallas_call(
        functools.partial(_mla_attn_kernel, tk=t),
        grid=(batch, MLA_HEADS // heads_per_step, nq),
        in_specs=[
            pl.BlockSpec((t, qk_width), lambda b, h, i: (b * nq + i, h)),
            pl.BlockSpec((None, seq, qk_width), lambda b, h, i: (b, 0, h)),
            pl.BlockSpec((None, seq, v_width), lambda b, h, i: (b, 0, h)),
        ],
        out_specs=pl.BlockSpec((t, v_width), lambda b, h, i: (b * nq + i, h)),
        out_shape=jax.ShapeDtypeStruct((M, MLA_HEADS * HEAD_DIM), F32),
        compiler_params=_params("arbitrary", "arbitrary", "arbitrary"),
        name="mla_attn",
    )(q, k3, v3)


def _sb_attn_kernel(q_ref, k_ref, v_ref, o_ref, *, tk, rq):
    qi = pl.program_id(2)
    tq = q_ref.shape[0]
    heads = q_ref.shape[1] // HEAD_DIM
    blocks_per_step = tq // tk
    chunks = [(hd, r) for hd in range(heads) for r in range(tq // rq)]
    jj = lax.broadcasted_iota(jnp.int32, (tk, tk), 0)
    ss = lax.broadcasted_iota(jnp.int32, (tk, tk), 1)
    suffix = (jj > ss).astype(BF16)
    sign_bit = jnp.int32(-2 ** 31)

    def step(first, carries, masked):
        row = lax.broadcasted_iota(jnp.int32, (rq, tk), 0)
        col = lax.broadcasted_iota(jnp.int32, (rq, tk), 1)
        tiles = []
        for d in reversed(range(blocks_per_step)):
            for u, (_, r) in enumerate(chunks):
                if masked and d * tk >= (r + 1) * rq - 1:
                    continue
                partial = masked and (d + 1) * tk - 1 >= r * rq
                tiles.append((u, d, (col + d * tk < row + r * rq) if partial else None))

        def kv_rows(d):
            return pl.ds(pl.multiple_of((first + d) * tk, tk), tk)

        def cols(u):
            hd = chunks[u][0]
            return slice(hd * HEAD_DIM, (hd + 1) * HEAD_DIM)

        runs = [run for run, _ in carries]
        accs = [acc for _, acc in carries]
        n = len(tiles)
        zs, log_betas, tails = [None] * n, [None] * n, [None] * n

        def scores(i):
            u, d, _ = tiles[i]
            r = chunks[u][1]
            zs[i] = lax.dot_general(q_ref[r * rq:(r + 1) * rq, cols(u)],
                                    k_ref[kv_rows(d), cols(u)],
                                    (((1,), (1,)), ((), ())), preferred_element_type=F32)

        def log_terms(i):
            u, d, mask = tiles[i]
            z = zs[i]
            neg_abs = pltpu.bitcast(pltpu.bitcast(z, jnp.int32) | sign_bit, F32)
            softplus = jnp.log(1.0 + jnp.exp2(neg_abs)) * LOG2_E
            log_beta = jnp.minimum(z, 0.0) - softplus
            log_om = log_beta - z
            if mask is not None:
                log_om = jnp.where(mask, log_om, 0.0)
            log_betas[i] = log_beta + runs[u]
            runs[u] = runs[u] + jnp.sum(log_om, axis=-1, keepdims=True)
            tails[i] = jnp.dot(log_om.astype(BF16), suffix, preferred_element_type=F32)

        def accumulate(i):
            u, d, mask = tiles[i]
            a = jnp.exp2(log_betas[i] + tails[i])
            if mask is not None:
                a = jnp.where(mask, a, 0.0)
            accs[u] = accs[u] + jnp.dot(a.astype(BF16), v_ref[kv_rows(d), cols(u)],
                                        preferred_element_type=F32)

        group, lag = 2, 1
        groups = [range(s, min(s + group, n)) for s in range(0, n, group)]
        for t in range(len(groups) + 1 + lag):
            for stage, back in ((scores, 0), (log_terms, 1), (accumulate, 1 + lag)):
                if 0 <= t - back < len(groups):
                    for i in groups[t - back]:
                        stage(i)
        return tuple(zip(runs, accs))

    def live(carries):
        top = functools.reduce(jnp.maximum, [run for run, _ in carries])
        return jnp.max(top) > SB_DEAD_LOG2

    init = tuple((jnp.zeros((rq, 1), F32), jnp.zeros((rq, HEAD_DIM), F32)) for _ in chunks)
    carries = step(qi * blocks_per_step, init, True)

    def body(state):
        n, _, carries = state
        carries = step((qi - 1 - n) * blocks_per_step, carries, False)
        return n + 1, live(carries), carries

    _, _, carries = lax.while_loop(lambda state: (state[0] < qi) & state[1], body,
                                   (jnp.int32(0), live(carries), carries))
    for (hd, r), (_, acc) in zip(chunks, carries):
        o_ref[r * rq:(r + 1) * rq, hd * HEAD_DIM:(hd + 1) * HEAD_DIM] = acc


def _sb_attention(q, k, v, *, batch, seq, tq=512, tk=256, rq=128, heads_per_step=2):
    M = q.shape[0]
    t = min(tq, seq)
    tk = min(tk, t)
    rq = min(rq, t)
    nq = seq // t
    width = heads_per_step * HEAD_DIM
    k3 = k.reshape(batch, seq, k.shape[1])
    v3 = v.reshape(batch, seq, v.shape[1])
    return pl.pallas_call(
        functools.partial(_sb_attn_kernel, tk=tk, rq=rq),
        grid=(batch, SB_HEADS // heads_per_step, nq),
        in_specs=[
            pl.BlockSpec((t, width), lambda b, h, i: (b * nq + i, h)),
            pl.BlockSpec((None, seq, width), lambda b, h, i: (b, 0, h)),
            pl.BlockSpec((None, seq, width), lambda b, h, i: (b, 0, h)),
        ],
        out_specs=pl.BlockSpec((t, width), lambda b, h, i: (b * nq + i, h)),
        out_shape=jax.ShapeDtypeStruct((M, SB_HEADS * HEAD_DIM), F32),
        compiler_params=_params("arbitrary", "arbitrary", "arbitrary"),
        name="sb_attn",
    )(q, k3, v3)


def _outproj_kernel(x_ref, om_ref, os_ref, gt_ref, mg_ref, sg_ref, wom_ref, wos_ref,
                    lng_ref, lnb_ref, o_ref, *, alpha):
    om = _rms_norm(om_ref[...], mg_ref[...]).astype(BF16)
    os_ = _rms_norm(os_ref[...], sg_ref[...]).astype(BF16)
    m = (jnp.dot(om, wom_ref[...], preferred_element_type=F32)
         + jnp.dot(os_, wos_ref[...], preferred_element_type=F32))
    y = alpha * x_ref[...] + (1.0 + gt_ref[...]) * m
    o_ref[...] = _layer_norm(y, lng_ref[...], lnb_ref[...])


def _outproj_block(x, o_mla, o_sb, ada, l, k_gate, w, ln_g, ln_b, k_ln, *, alpha,
                   rows_per_batch, tm=512):
    M, D = x.shape
    tm = min(tm, rows_per_batch)
    tiles_per_batch = rows_per_batch // tm

    def rows(width):
        return pl.BlockSpec((tm, width), lambda i: (i, 0))

    consts = [w["mla_out_g"], w["sb_out_g"], w["o_mla"], w["o_sb"]]
    ln_spec = pl.BlockSpec((None, None, 1, D), lambda i: (l, k_ln, 0, 0))
    return pl.pallas_call(
        functools.partial(_outproj_kernel, alpha=alpha),
        grid=(M // tm,),
        in_specs=[rows(D), rows(o_mla.shape[1]), rows(o_sb.shape[1]),
                  pl.BlockSpec((None, None, None, 1, D),
                               lambda i: (l, i // tiles_per_batch, k_gate, 0, 0))]
                 + [_layer_resident(a, l) for a in consts] + [ln_spec, ln_spec],
        out_specs=rows(D),
        out_shape=jax.ShapeDtypeStruct((M, D), F32),
        compiler_params=_params("arbitrary"),
        name="outproj",
    )(x, o_mla, o_sb, ada, *consts, ln_g, ln_b)


def _mixer_weights(w_in, q_norm_g, kv_norm_g, w_uq, w_ukv, mla_out_g, sb_out_g, w_o):
    depth = w_in.shape[0]
    o1 = Q_LORA_RANK
    o2 = o1 + KV_LORA_RANK
    o3 = o2 + QK_ROPE_DIM
    lane_pad = ((0, 0),) * 2 + ((0, LANES - QK_ROPE_DIM),)
    uq = w_uq.astype(BF16).reshape(depth, Q_LORA_RANK, MLA_HEADS, MLA_QK_DIM)
    uq_rope = jnp.pad(uq[..., QK_NOPE_DIM:], ((0, 0),) + lane_pad)
    ukv = w_ukv.astype(BF16).reshape(depth, KV_LORA_RANK, MLA_HEADS, QK_NOPE_DIM + HEAD_DIM)
    mla_w = MLA_HEADS * HEAD_DIM
    return {
        "cq": w_in[:, :, :o1].astype(BF16),
        "ckv": w_in[:, :, o1:o2].astype(BF16),
        "kr": jnp.pad(w_in[:, :, o2:o3].astype(BF16), lane_pad),
        "sb": w_in[:, :, o3:].astype(BF16),
        "qn_g": q_norm_g.reshape(depth, 1, -1),
        "kvn_g": kv_norm_g.reshape(depth, 1, -1),
        "uq_nope": uq[..., :QK_NOPE_DIM].reshape(depth, Q_LORA_RANK, MLA_HEADS * QK_NOPE_DIM),
        "uq_rope": uq_rope.reshape(depth, Q_LORA_RANK, MLA_HEADS * LANES),
        "uk": ukv[..., :QK_NOPE_DIM].reshape(depth, KV_LORA_RANK, MLA_HEADS * QK_NOPE_DIM),
        "uv": ukv[..., QK_NOPE_DIM:].reshape(depth, KV_LORA_RANK, MLA_HEADS * HEAD_DIM),
        "mla_out_g": mla_out_g.reshape(depth, 1, -1),
        "sb_out_g": sb_out_g.reshape(depth, 1, -1),
        "o_mla": w_o[:, :mla_w].astype(BF16),
        "o_sb": w_o[:, mla_w:].astype(BF16),
    }


def kernel(x, c, positions, ada_w, ada_b, ln_g, ln_b, ffn1_wi, ffn1_wo, w_in, q_norm_g, kv_norm_g, w_uq, w_ukv, mla_out_g, sb_out_g, w_o, ffn2_wi, ffn2_wo):
    B, S, D = x.shape
    depth = ada_w.shape[0]
    alpha = (2.0 * depth) ** 0.25
    M = B * S

    ada = _ada_all_layers(c, ada_w, ada_b).reshape(depth, B, N_ADA, 1, D)
    tables = _rope_tables(positions)
    ln_g = ln_g.reshape(depth, 3, 1, D)
    ln_b = ln_b.reshape(depth, 3, 1, D)
    ffn1 = (ffn1_wi.astype(BF16), ffn1_wo.astype(BF16))
    ffn2 = (ffn2_wi.astype(BF16), ffn2_wo.astype(BF16))
    w = _mixer_weights(w_in, q_norm_g, kv_norm_g, w_uq, w_ukv, mla_out_g, sb_out_g, w_o)
    xf = x.reshape(M, D)
    for l in range(depth):
        xf = _ffn_block(xf, ada, l, 0, *ffn1, ln_g, ln_b, 0, alpha=alpha, rows_per_batch=S)
        q, k, v, sq, sk, sv = _inproj_block(xf, ada, l, 3, tables, w, rows_per_batch=S)
        o_mla = _mla_attention(q, k, v, batch=B, seq=S)
        o_sb = _sb_attention(sq, sk, sv, batch=B, seq=S)
        xf = _outproj_block(xf, o_mla, o_sb, ada, l, 5, w, ln_g, ln_b, 1,
                            alpha=alpha, rows_per_batch=S)
        xf = _ffn_block(xf, ada, l, 6, *ffn2, ln_g, ln_b, 2, alpha=alpha, rows_per_batch=S)
    return xf.reshape(B, S, D)
```

```python
import functools

import jax
import jax.numpy as jnp
from jax import lax
from jax.experimental import pallas as pl
from jax.experimental.pallas import tpu as pltpu

HEAD_DIM = 128
MLA_HEADS = 8
SB_HEADS = 8
Q_LORA_RANK = 768
KV_LORA_RANK = 512
QK_NOPE_DIM = 128
QK_ROPE_DIM = 64
MLA_QK_DIM = QK_NOPE_DIM + QK_ROPE_DIM
ROPE_THETA = 10000.0
N_ADA = 9
LN_EPS = 1e-5
RMS_EPS = 1e-6
FFN_RESIDUAL_WEIGHT = 0.5
LOG2_E = 1.4426950408889634
SB_DEAD_LOG2 = -152.0

LANES = 128
MLA_QK_PAD = 2 * LANES
VMEM_LIMIT_BYTES = 60 * 1024 * 1024
OUTPROJ_ROW_CHUNK = 256

F32 = jnp.float32
BF16 = jnp.bfloat16


def _params(*semantics):
    return pltpu.CompilerParams(dimension_semantics=semantics,
                                vmem_limit_bytes=VMEM_LIMIT_BYTES)


def _layer_resident(stacked, l):
    index = (l,) + (0,) * (stacked.ndim - 1)
    return pl.BlockSpec((None,) + stacked.shape[1:], lambda *_: index,
                        pipeline_mode=pl.Buffered(1))


def _layer_norm(y, g, b, eps=LN_EPS):
    mu = jnp.mean(y, axis=-1, keepdims=True)
    d = y - mu
    var = jnp.mean(d * d, axis=-1, keepdims=True)
    return d * lax.rsqrt(var + eps) * g + b


def _rms_norm(y, g):
    return y * lax.rsqrt(jnp.mean(y * y, axis=-1, keepdims=True) + RMS_EPS) * g


def _ada_kernel(c_ref, w_ref, b_ref, o_ref):
    c = c_ref[...]
    c_act = (c * jax.nn.sigmoid(c)).astype(BF16)
    acc = jnp.dot(c_act, w_ref[...].astype(BF16), preferred_element_type=F32)
    o_ref[...] = acc + b_ref[...]


def _ada_all_layers(c, ada_w, ada_b, tn=1024):
    L, D, N = ada_w.shape
    B = c.shape[0]
    tn = min(tn, D)
    assert D % tn == 0
    return pl.pallas_call(
        _ada_kernel,
        grid=(L, N // tn),
        in_specs=[
            pl.BlockSpec((B, D), lambda l, j: (0, 0)),
            pl.BlockSpec((None, D, tn), lambda l, j: (l, 0, j)),
            pl.BlockSpec((None, 1, tn), lambda l, j: (l, 0, j)),
        ],
        out_specs=pl.BlockSpec((None, B, tn), lambda l, j: (l, 0, j)),
        out_shape=jax.ShapeDtypeStruct((L, B, N), F32),
        compiler_params=_params("arbitrary", "arbitrary"),
        name="ada",
    )(c, ada_w, ada_b.reshape(L, 1, N))


def _rope_table_kernel(pos_ref, invf_ref, cos_ref, sinlo_ref, sinhi_ref):
    half = QK_ROPE_DIM // 2
    ang = pos_ref[...].astype(F32) * invf_ref[...]
    lane = lax.broadcasted_iota(jnp.int32, ang.shape, 1)
    cos = jnp.cos(ang)
    sin = jnp.sin(ang)
    cos_ref[...] = jnp.where(lane < QK_ROPE_DIM, cos, 0.0)
    sinlo_ref[...] = jnp.where(lane < half, -sin, 0.0)
    sinhi_ref[...] = jnp.where((lane >= half) & (lane < QK_ROPE_DIM), sin, 0.0)


def _rope_tables(positions, tm=1024):
    M = positions.size
    half = QK_ROPE_DIM // 2
    inv_freq = ROPE_THETA ** (-jnp.arange(half, dtype=F32) / half)
    invf = jnp.tile(inv_freq, LANES // half).reshape(1, LANES)
    tm = min(tm, M)
    spec = pl.BlockSpec((tm, LANES), lambda i: (i, 0))
    shape = jax.ShapeDtypeStruct((M, LANES), F32)
    return pl.pallas_call(
        _rope_table_kernel,
        grid=(M // tm,),
        in_specs=[pl.BlockSpec((tm, 1), lambda i: (i, 0)),
                  pl.BlockSpec((1, LANES), lambda i: (0, 0))],
        out_specs=[spec, spec, spec],
        out_shape=[shape, shape, shape],
        compiler_params=_params("arbitrary"),
        name="rope_tables",
    )(positions.reshape(M, 1), invf)


def _ffn_kernel(x_ref, sh_ref, sc_ref, gt_ref, wg_ref, wu_ref, wo_ref, lng_ref, lnb_ref,
                o_ref, *, alpha):
    j = pl.program_id(1)

    @pl.when(j == 0)
    def _():
        o_ref[...] = jnp.zeros_like(o_ref)

    h = (x_ref[...] * (1.0 + sc_ref[...]) + sh_ref[...]).astype(BF16)
    g = jnp.dot(h, wg_ref[...], preferred_element_type=F32)
    u = jnp.dot(h, wu_ref[...], preferred_element_type=F32)
    a = (g * jax.nn.sigmoid(g) * u).astype(BF16)
    o_ref[...] += jnp.dot(a, wo_ref[...], preferred_element_type=F32)

    @pl.when(j == pl.num_programs(1) - 1)
    def _():
        c = (FFN_RESIDUAL_WEIGHT / alpha) * (1.0 + gt_ref[...])
        y = x_ref[...] + c * o_ref[...]
        o_ref[...] = _layer_norm(y, lng_ref[...], lnb_ref[...], LN_EPS / (alpha * alpha))


def _ffn_block(x, ada, l, k_shift, wi, wo, ln_g, ln_b, k_ln, *, alpha, rows_per_batch,
               tm=1024, tf=512):
    M, D = x.shape
    F = wo.shape[1]
    tm = min(tm, rows_per_batch)
    tf = min(tf, F)
    nf = F // tf
    tiles_per_batch = rows_per_batch // tm

    def ada_spec(k):
        return pl.BlockSpec((None, None, None, 1, D),
                            lambda i, j: (l, i // tiles_per_batch, k, 0, 0))

    ln_spec = pl.BlockSpec((None, None, 1, D), lambda i, j: (l, k_ln, 0, 0))
    return pl.pallas_call(
        functools.partial(_ffn_kernel, alpha=alpha),
        grid=(M // tm, nf),
        in_specs=[
            pl.BlockSpec((tm, D), lambda i, j: (i, 0)),
            ada_spec(k_shift), ada_spec(k_shift + 1), ada_spec(k_shift + 2),
            pl.BlockSpec((None, D, tf), lambda i, j: (l, 0, j)),
            pl.BlockSpec((None, D, tf), lambda i, j: (l, 0, j + nf)),
            pl.BlockSpec((None, tf, D), lambda i, j: (l, j, 0)),
            ln_spec, ln_spec,
        ],
        out_specs=pl.BlockSpec((tm, D), lambda i, j: (i, 0)),
        out_shape=jax.ShapeDtypeStruct((M, D), F32),
        compiler_params=_params("arbitrary", "arbitrary"),
        name="ffn",
    )(x, ada, ada, ada, wi, wi, wo, ln_g, ln_b)


def _inproj_kernel(x_ref, sh_ref, sc_ref, cos_ref, sinlo_ref, sinhi_ref,
                   wcq_ref, wckv_ref, wkr_ref, wsb_ref, qng_ref, kvng_ref,
                   wuqn_ref, wuqr_ref, wuk_ref, wuv_ref,
                   q_ref, k_ref, v_ref, sq_ref, sk_ref, sv_ref):
    h = (x_ref[...] * (1.0 + sc_ref[...]) + sh_ref[...]).astype(BF16)
    cos, sin_lo, sin_hi = cos_ref[...], sinlo_ref[...], sinhi_ref[...]
    half = QK_ROPE_DIM // 2

    def rope(xr):
        return (xr * cos + pltpu.roll(xr, LANES - half, axis=1) * sin_lo
                + pltpu.roll(xr, half, axis=1) * sin_hi)

    c_q = jnp.dot(h, wcq_ref[...], preferred_element_type=F32)
    c_kv = jnp.dot(h, wckv_ref[...], preferred_element_type=F32)
    k_rope = jnp.dot(h, wkr_ref[...], preferred_element_type=F32)

    sb_w = SB_HEADS * HEAD_DIM
    sb = jnp.dot(h, wsb_ref[...], preferred_element_type=F32)
    sq_ref[...] = (sb[:, :sb_w] * (LOG2_E * HEAD_DIM ** -0.5)).astype(BF16)
    sk_ref[...] = sb[:, sb_w:2 * sb_w].astype(BF16)
    sv_ref[...] = sb[:, 2 * sb_w:].astype(BF16)

    c_qn = _rms_norm(c_q, qng_ref[...]).astype(BF16)
    c_kvn = _rms_norm(c_kv, kvng_ref[...]).astype(BF16)
    q_nope = jnp.dot(c_qn, wuqn_ref[...], preferred_element_type=F32)
    q_rope = jnp.dot(c_qn, wuqr_ref[...], preferred_element_type=F32)
    k_nope = jnp.dot(c_kvn, wuk_ref[...], preferred_element_type=F32)
    v_ref[...] = jnp.dot(c_kvn, wuv_ref[...], preferred_element_type=F32).astype(BF16)
    q_scale = LOG2_E * MLA_QK_DIM ** -0.5
    for hd in range(MLA_HEADS):
        lo = hd * MLA_QK_PAD
        q_ref[:, lo:lo + LANES] = (q_nope[:, hd * LANES:(hd + 1) * LANES] * q_scale).astype(BF16)
        q_ref[:, lo + LANES:lo + 2 * LANES] = (
            rope(q_rope[:, hd * LANES:(hd + 1) * LANES]) * q_scale).astype(BF16)

    k_pe = rope(k_rope).astype(BF16)
    for hd in range(MLA_HEADS):
        lo = hd * MLA_QK_PAD
        k_ref[:, lo:lo + LANES] = k_nope[:, hd * LANES:(hd + 1) * LANES].astype(BF16)
        k_ref[:, lo + LANES:lo + 2 * LANES] = k_pe


def _inproj_block(x, ada, l, k_shift, tables, w, *, rows_per_batch, tm=256):
    M, D = x.shape
    tm = min(tm, rows_per_batch)
    tiles_per_batch = rows_per_batch // tm

    def ada_spec(k):
        return pl.BlockSpec((None, None, None, 1, D),
                            lambda i: (l, i // tiles_per_batch, k, 0, 0))

    def rows(width):
        return pl.BlockSpec((tm, width), lambda i: (i, 0))

    weights = [w["cq"], w["ckv"], w["kr"], w["sb"], w["qn_g"], w["kvn_g"],
               w["uq_nope"], w["uq_rope"], w["uk"], w["uv"]]
    widths = [MLA_HEADS * MLA_QK_PAD, MLA_HEADS * MLA_QK_PAD, MLA_HEADS * HEAD_DIM,
              SB_HEADS * HEAD_DIM, SB_HEADS * HEAD_DIM, SB_HEADS * HEAD_DIM]
    return pl.pallas_call(
        _inproj_kernel,
        grid=(M // tm,),
        in_specs=[rows(D), ada_spec(k_shift), ada_spec(k_shift + 1),
                  rows(LANES), rows(LANES), rows(LANES)]
                 + [_layer_resident(a, l) for a in weights],
        out_specs=[rows(n) for n in widths],
        out_shape=[jax.ShapeDtypeStruct((M, n), BF16) for n in widths],
        compiler_params=_params("arbitrary"),
        name="inproj",
    )(x, ada, ada, *tables, *weights)


def _mla_attn_kernel(q_ref, k_ref, v_ref, o_ref, *, tk):
    qi = pl.program_id(2)
    tq = q_ref.shape[0]
    heads = q_ref.shape[1] // MLA_QK_PAD

    def step(c, carries, mask):
        kv_rows = pl.ds(pl.multiple_of(c * tk, tk), tk)
        scores = [lax.dot_general(q_ref[:, hd * MLA_QK_PAD:(hd + 1) * MLA_QK_PAD],
                                  k_ref[kv_rows, hd * MLA_QK_PAD:(hd + 1) * MLA_QK_PAD],
                                  (((1,), (1,)), ((), ())), preferred_element_type=F32)
                  for hd in range(heads)]
        stats, probs = [], []
        for s, (m, l, _) in zip(scores, carries):
            if mask is not None:
                s = jnp.where(mask, s, -jnp.inf)
            m_new = jnp.maximum(m, jnp.max(s, axis=-1, keepdims=True))
            corr = jnp.exp2(m - m_new)
            p = jnp.exp2(s - m_new)
            stats.append((m_new, l * corr + jnp.sum(p, axis=-1, keepdims=True), corr))
            probs.append(p.astype(BF16))
        out = []
        for hd, (p, (m_new, l_new, corr), (_, _, acc)) in enumerate(zip(probs, stats, carries)):
            pv = jnp.dot(p, v_ref[kv_rows, hd * HEAD_DIM:(hd + 1) * HEAD_DIM],
                         preferred_element_type=F32)
            out.append((m_new, l_new, acc * corr + pv))
        return tuple(out)

    row = lax.broadcasted_iota(jnp.int32, (tq, tk), 0)
    col = lax.broadcasted_iota(jnp.int32, (tq, tk), 1)
    init = tuple((jnp.full((tq, 1), -jnp.inf, F32), jnp.zeros((tq, 1), F32),
                  jnp.zeros((tq, HEAD_DIM), F32)) for _ in range(heads))
    carries = step(qi, init, col <= row)
    carries = lax.fori_loop(0, qi, lambda c, cs: step(c, cs, None), carries)
    for hd, (_, l, acc) in enumerate(carries):
        o_ref[:, hd * HEAD_DIM:(hd + 1) * HEAD_DIM] = acc / l


def _mla_attention(q, k, v, *, batch, seq, t=512, heads_per_step=2):
    M = q.shape[0]
    t = min(t, seq)
    nq = seq // t
    k3 = k.reshape(batch, seq, k.shape[1])
    v3 = v.reshape(batch, seq, v.shape[1])
    qk_width = heads_per_step * MLA_QK_PAD
    v_width = heads_per_step * HEAD_DIM
    return pl.pallas_call(
        functools.partial(_mla_attn_kernel, tk=t),
        grid=(batch, MLA_HEADS // heads_per_step, nq),
        in_specs=[
            pl.BlockSpec((t, qk_width), lambda b, h, i: (b * nq + i, h)),
            pl.BlockSpec((None, seq, qk_width), lambda b, h, i: (b, 0, h)),
            pl.BlockSpec((None, seq, v_width), lambda b, h, i: (b, 0, h)),
        ],
        out_specs=pl.BlockSpec((t, v_width), lambda b, h, i: (b * nq + i, h)),
        out_shape=jax.ShapeDtypeStruct((M, MLA_HEADS * HEAD_DIM), F32),
        compiler_params=_params("arbitrary", "arbitrary", "arbitrary"),
        name="mla_attn",
    )(q, k3, v3)


def _sb_attn_kernel(q_ref, k_ref, v_ref, o_ref, *, tk, rq):
    qi = pl.program_id(2)
    tq = q_ref.shape[0]
    heads = q_ref.shape[1] // HEAD_DIM
    blocks_per_step = tq // tk
    chunks = [(hd, r) for hd in range(heads) for r in range(tq // rq)]
    jj = lax.broadcasted_iota(jnp.int32, (tk, tk), 0)
    ss = lax.broadcasted_iota(jnp.int32, (tk, tk), 1)
    suffix = (jj > ss).astype(BF16)
    sign_bit = jnp.int32(-2 ** 31)

    def step(first, carries, masked):
        row = lax.broadcasted_iota(jnp.int32, (rq, tk), 0)
        col = lax.broadcasted_iota(jnp.int32, (rq, tk), 1)
        tiles = []
        for d in reversed(range(blocks_per_step)):
            for u, (_, r) in enumerate(chunks):
                if masked and d * tk >= (r + 1) * rq - 1:
                    continue
                partial = masked and (d + 1) * tk - 1 >= r * rq
                tiles.append((u, d, (col + d * tk < row + r * rq) if partial else None))

        def kv_rows(d):
            return pl.ds(pl.multiple_of((first + d) * tk, tk), tk)

        def cols(u):
            hd = chunks[u][0]
            return slice(hd * HEAD_DIM, (hd + 1) * HEAD_DIM)

        runs = [run for run, _ in carries]
        accs = [acc for _, acc in carries]
        n = len(tiles)
        zs, log_betas, tails = [None] * n, [None] * n, [None] * n

        def scores(i):
            u, d, _ = tiles[i]
            r = chunks[u][1]
            zs[i] = lax.dot_general(q_ref[r * rq:(r + 1) * rq, cols(u)],
                                    k_ref[kv_rows(d), cols(u)],
                                    (((1,), (1,)), ((), ())), preferred_element_type=F32)

        def log_terms(i):
            u, d, mask = tiles[i]
            z = zs[i]
            neg_abs = pltpu.bitcast(pltpu.bitcast(z, jnp.int32) | sign_bit, F32)
            softplus = jnp.log(1.0 + jnp.exp2(neg_abs)) * LOG2_E
            log_beta = jnp.minimum(z, 0.0) - softplus
            log_om = log_beta - z
            if mask is not None:
                log_om = jnp.where(mask, log_om, 0.0)
            log_betas[i] = log_beta + runs[u]
            runs[u] = runs[u] + jnp.sum(log_om, axis=-1, keepdims=True)
            tails[i] = jnp.dot(log_om.astype(BF16), suffix, preferred_element_type=F32)

        def accumulate(i):
            u, d, mask = tiles[i]
            a = jnp.exp2(log_betas[i] + tails[i])
            if mask is not None:
                a = jnp.where(mask, a, 0.0)
            accs[u] = accs[u] + jnp.dot(a.astype(BF16), v_ref[kv_rows(d), cols(u)],
                                        preferred_element_type=F32)

        group, lag = 2, 1
        groups = [range(s, min(s + group, n)) for s in range(0, n, group)]
        for t in range(len(groups) + 1 + lag):
            for stage, back in ((scores, 0), (log_terms, 1), (accumulate, 1 + lag)):
                if 0 <= t - back < len(groups):
                    for i in groups[t - back]:
                        stage(i)
        return tuple(zip(runs, accs))

    def live(carries):
        top = functools.reduce(jnp.maximum, [run for run, _ in carries])
        return jnp.max(top) > SB_DEAD_LOG2

    init = tuple((jnp.zeros((rq, 1), F32), jnp.zeros((rq, HEAD_DIM), F32)) for _ in chunks)
    carries = step(qi * blocks_per_step, init, True)

    def body(state):
        n, _, carries = state
        carries = step((qi - 1 - n) * blocks_per_step, carries, False)
        return n + 1, live(carries), carries

    _, _, carries = lax.while_loop(lambda state: (state[0] < qi) & state[1], body,
                                   (jnp.int32(0), live(carries), carries))
    for (hd, r), (_, acc) in zip(chunks, carries):
        o_ref[r * rq:(r + 1) * rq, hd * HEAD_DIM:(hd + 1) * HEAD_DIM] = acc


def _sb_attention(q, k, v, *, batch, seq, tq=512, tk=256, rq=128, heads_per_step=2):
    M = q.shape[0]
    t = min(tq, seq)
    tk = min(tk, t)
    rq = min(rq, t)
    nq = seq // t
    width = heads_per_step * HEAD_DIM
    k3 = k.reshape(batch, seq, k.shape[1])
    v3 = v.reshape(batch, seq, v.shape[1])
    return pl.pallas_call(
        functools.partial(_sb_attn_kernel, tk=tk, rq=rq),
        grid=(batch, SB_HEADS // heads_per_step, nq),
        in_specs=[
            pl.BlockSpec((t, width), lambda b, h, i: (b * nq + i, h)),
            pl.BlockSpec((None, seq, width), lambda b, h, i: (b, 0, h)),
            pl.BlockSpec((None, seq, width), lambda b, h, i: (b, 0, h)),
        ],
        out_specs=pl.BlockSpec((t, width), lambda b, h, i: (b * nq + i, h)),
        out_shape=jax.ShapeDtypeStruct((M, SB_HEADS * HEAD_DIM), F32),
        compiler_params=_params("arbitrary", "arbitrary", "arbitrary"),
        name="sb_attn",
    )(q, k3, v3)


def _outproj_kernel(x_ref, om_ref, os_ref, gt_ref, mg_ref, sg_ref, wom_ref, wos_ref,
                    lng_ref, lnb_ref, o_ref, *, alpha):
    rows_total = x_ref.shape[0]
    chunk = min(OUTPROJ_ROW_CHUNK, rows_total)
    gate = (1.0 + gt_ref[...]) / alpha
    for r0 in range(0, rows_total, chunk):
        rows = slice(r0, r0 + chunk)
        om = _rms_norm(om_ref[rows, :], mg_ref[...]).astype(BF16)
        os_ = _rms_norm(os_ref[rows, :], sg_ref[...]).astype(BF16)
        m = (jnp.dot(om, wom_ref[...], preferred_element_type=F32)
             + jnp.dot(os_, wos_ref[...], preferred_element_type=F32))
        y = x_ref[rows, :] + gate * m
        o_ref[rows, :] = _layer_norm(y, lng_ref[...], lnb_ref[...], LN_EPS / (alpha * alpha))


def _outproj_block(x, o_mla, o_sb, ada, l, k_gate, w, ln_g, ln_b, k_ln, *, alpha,
                   rows_per_batch, tm=512):
    M, D = x.shape
    tm = min(tm, rows_per_batch)
    tiles_per_batch = rows_per_batch // tm

    def rows(width):
        return pl.BlockSpec((tm, width), lambda i: (i, 0))

    consts = [w["mla_out_g"], w["sb_out_g"], w["o_mla"], w["o_sb"]]
    ln_spec = pl.BlockSpec((None, None, 1, D), lambda i: (l, k_ln, 0, 0))
    return pl.pallas_call(
        functools.partial(_outproj_kernel, alpha=alpha),
        grid=(M // tm,),
        in_specs=[rows(D), rows(o_mla.shape[1]), rows(o_sb.shape[1]),
                  pl.BlockSpec((None, None, None, 1, D),
                               lambda i: (l, i // tiles_per_batch, k_gate, 0, 0))]
                 + [_layer_resident(a, l) for a in consts] + [ln_spec, ln_spec],
        out_specs=rows(D),
        out_shape=jax.ShapeDtypeStruct((M, D), F32),
        compiler_params=_params("arbitrary"),
        name="outproj",
    )(x, o_mla, o_sb, ada, *consts, ln_g, ln_b)


def _mixer_weights(w_in, q_norm_g, kv_norm_g, w_uq, w_ukv, mla_out_g, sb_out_g, w_o):
    depth = w_in.shape[0]
    o1 = Q_LORA_RANK
    o2 = o1 + KV_LORA_RANK
    o3 = o2 + QK_ROPE_DIM
    lane_pad = ((0, 0),) * 2 + ((0, LANES - QK_ROPE_DIM),)
    uq = w_uq.astype(BF16).reshape(depth, Q_LORA_RANK, MLA_HEADS, MLA_QK_DIM)
    uq_rope = jnp.pad(uq[..., QK_NOPE_DIM:], ((0, 0),) + lane_pad)
    ukv = w_ukv.astype(BF16).reshape(depth, KV_LORA_RANK, MLA_HEADS, QK_NOPE_DIM + HEAD_DIM)
    mla_w = MLA_HEADS * HEAD_DIM
    return {
        "cq": w_in[:, :, :o1].astype(BF16),
        "ckv": w_in[:, :, o1:o2].astype(BF16),
        "kr": jnp.pad(w_in[:, :, o2:o3].astype(BF16), lane_pad),
        "sb": w_in[:, :, o3:].astype(BF16),
        "qn_g": q_norm_g.reshape(depth, 1, -1),
        "kvn_g": kv_norm_g.reshape(depth, 1, -1),
        "uq_nope": uq[..., :QK_NOPE_DIM].reshape(depth, Q_LORA_RANK, MLA_HEADS * QK_NOPE_DIM),
        "uq_rope": uq_rope.reshape(depth, Q_LORA_RANK, MLA_HEADS * LANES),
        "uk": ukv[..., :QK_NOPE_DIM].reshape(depth, KV_LORA_RANK, MLA_HEADS * QK_NOPE_DIM),
        "uv": ukv[..., QK_NOPE_DIM:].reshape(depth, KV_LORA_RANK, MLA_HEADS * HEAD_DIM),
        "mla_out_g": mla_out_g.reshape(depth, 1, -1),
        "sb_out_g": sb_out_g.reshape(depth, 1, -1),
        "o_mla": w_o[:, :mla_w].astype(BF16),
        "o_sb": w_o[:, mla_w:].astype(BF16),
    }


def kernel(x, c, positions, ada_w, ada_b, ln_g, ln_b, ffn1_wi, ffn1_wo, w_in, q_norm_g, kv_norm_g, w_uq, w_ukv, mla_out_g, sb_out_g, w_o, ffn2_wi, ffn2_wo):
    B, S, D = x.shape
    depth = ada_w.shape[0]
    alpha = (2.0 * depth) ** 0.25
    M = B * S

    ada = _ada_all_layers(c, ada_w, ada_b).reshape(depth, B, N_ADA, 1, D)
    tables = _rope_tables(positions)
    ln_g = ln_g.reshape(depth, 3, 1, D)
    ln_b = ln_b.reshape(depth, 3, 1, D)
    ffn1 = (ffn1_wi.astype(BF16), ffn1_wo.astype(BF16))
    ffn2 = (ffn2_wi.astype(BF16), ffn2_wo.astype(BF16))
    w = _mixer_weights(w_in, q_norm_g, kv_norm_g, w_uq, w_ukv, mla_out_g, sb_out_g, w_o)
    xf = x.reshape(M, D)
    for l in range(depth):
        xf = _ffn_block(xf, ada, l, 0, *ffn1, ln_g, ln_b, 0, alpha=alpha, rows_per_batch=S)
        q, k, v, sq, sk, sv = _inproj_block(xf, ada, l, 3, tables, w, rows_per_batch=S)
        o_mla = _mla_attention(q, k, v, batch=B, seq=S)
        o_sb = _sb_attention(sq, sk, sv, batch=B, seq=S)
        xf = _outproj_block(xf, o_mla, o_sb, ada, l, 5, w, ln_g, ln_b, 1,
                            alpha=alpha, rows_per_batch=S)
        xf = _ffn_block(xf, ada, l, 6, *ffn2, ln_g, ln_b, 2, alpha=alpha, rows_per_batch=S)
    return xf.reshape(B, S, D)
```

```python
import functools

import jax
import jax.numpy as jnp
from jax import lax
from jax.experimental import pallas as pl
from jax.experimental.pallas import tpu as pltpu

HEAD_DIM = 128
MLA_HEADS = 8
SB_HEADS = 8
Q_LORA_RANK = 768
KV_LORA_RANK = 512
QK_NOPE_DIM = 128
QK_ROPE_DIM = 64
MLA_QK_DIM = QK_NOPE_DIM + QK_ROPE_DIM
ROPE_THETA = 10000.0
N_ADA = 9
LN_EPS = 1e-5
RMS_EPS = 1e-6
FFN_RESIDUAL_WEIGHT = 0.5
LOG2_E = 1.4426950408889634
SB_DEAD_LOG2 = -152.0

LANES = 128
MLA_QK_PAD = 2 * LANES
VMEM_LIMIT_BYTES = 60 * 1024 * 1024
OUTPROJ_ROW_CHUNK = 256

F32 = jnp.float32
BF16 = jnp.bfloat16


def _params(*semantics):
    return pltpu.CompilerParams(dimension_semantics=semantics,
                                vmem_limit_bytes=VMEM_LIMIT_BYTES)


def _layer_resident(stacked, l):
    index = (l,) + (0,) * (stacked.ndim - 1)
    return pl.BlockSpec((None,) + stacked.shape[1:], lambda *_: index,
                        pipeline_mode=pl.Buffered(1))


def _layer_norm(y, g, b, eps=LN_EPS):
    mu = jnp.mean(y, axis=-1, keepdims=True)
    d = y - mu
    var = jnp.mean(d * d, axis=-1, keepdims=True)
    return d * lax.rsqrt(var + eps) * g + b


def _rms_norm(y, g):
    return y * lax.rsqrt(jnp.mean(y * y, axis=-1, keepdims=True) + RMS_EPS) * g


def _ada_kernel(c_ref, w_ref, b_ref, o_ref):
    c = c_ref[...]
    c_act = (c * jax.nn.sigmoid(c)).astype(BF16)
    acc = jnp.dot(c_act, w_ref[...].astype(BF16), preferred_element_type=F32)
    o_ref[...] = acc + b_ref[...]


def _ada_all_layers(c, ada_w, ada_b, tn=1024):
    L, D, N = ada_w.shape
    B = c.shape[0]
    tn = min(tn, D)
    assert D % tn == 0
    return pl.pallas_call(
        _ada_kernel,
        grid=(L, N // tn),
        in_specs=[
            pl.BlockSpec((B, D), lambda l, j: (0, 0)),
            pl.BlockSpec((None, D, tn), lambda l, j: (l, 0, j)),
            pl.BlockSpec((None, 1, tn), lambda l, j: (l, 0, j)),
        ],
        out_specs=pl.BlockSpec((None, B, tn), lambda l, j: (l, 0, j)),
        out_shape=jax.ShapeDtypeStruct((L, B, N), F32),
        compiler_params=_params("arbitrary", "arbitrary"),
        name="ada",
    )(c, ada_w, ada_b.reshape(L, 1, N))


def _rope_table_kernel(pos_ref, invf_ref, cos_ref, sinlo_ref, sinhi_ref):
    half = QK_ROPE_DIM // 2
    ang = pos_ref[...].astype(F32) * invf_ref[...]
    lane = lax.broadcasted_iota(jnp.int32, ang.shape, 1)
    cos = jnp.cos(ang)
    sin = jnp.sin(ang)
    cos_ref[...] = jnp.where(lane < QK_ROPE_DIM, cos, 0.0)
    sinlo_ref[...] = jnp.where(lane < half, -sin, 0.0)
    sinhi_ref[...] = jnp.where((lane >= half) & (lane < QK_ROPE_DIM), sin, 0.0)


def _rope_tables(positions, tm=1024):
    M = positions.size
    half = QK_ROPE_DIM // 2
    inv_freq = ROPE_THETA ** (-jnp.arange(half, dtype=F32) / half)
    invf = jnp.tile(inv_freq, LANES // half).reshape(1, LANES)
    tm = min(tm, M)
    spec = pl.BlockSpec((tm, LANES), lambda i: (i, 0))
    shape = jax.ShapeDtypeStruct((M, LANES), F32)
    return pl.pallas_call(
        _rope_table_kernel,
        grid=(M // tm,),
        in_specs=[pl.BlockSpec((tm, 1), lambda i: (i, 0)),
                  pl.BlockSpec((1, LANES), lambda i: (0, 0))],
        out_specs=[spec, spec, spec],
        out_shape=[shape, shape, shape],
        compiler_params=_params("arbitrary"),
        name="rope_tables",
    )(positions.reshape(M, 1), invf)


def _ffn_kernel(x_ref, sh_ref, sc_ref, gt_ref, wg_ref, wu_ref, wo_ref, lng_ref, lnb_ref,
                o_ref, *, alpha):
    j = pl.program_id(1)

    @pl.when(j == 0)
    def _():
        o_ref[...] = jnp.zeros_like(o_ref)

    h = (x_ref[...] * (1.0 + sc_ref[...]) + sh_ref[...]).astype(BF16)
    g = jnp.dot(h, wg_ref[...], preferred_element_type=F32)
    u = jnp.dot(h, wu_ref[...], preferred_element_type=F32)
    a = (g * jax.nn.sigmoid(g) * u).astype(BF16)
    o_ref[...] += jnp.dot(a, wo_ref[...], preferred_element_type=F32)

    @pl.when(j == pl.num_programs(1) - 1)
    def _():
        c = (FFN_RESIDUAL_WEIGHT / alpha) * (1.0 + gt_ref[...])
        y = x_ref[...] + c * o_ref[...]
        o_ref[...] = _layer_norm(y, lng_ref[...], lnb_ref[...], LN_EPS / (alpha * alpha))


def _ffn_block(x, ada, l, k_shift, wi, wo, ln_g, ln_b, k_ln, *, alpha, rows_per_batch,
               tm=1024, tf=512):
    M, D = x.shape
    F = wo.shape[1]
    tm = min(tm, rows_per_batch)
    tf = min(tf, F)
    nf = F // tf
    tiles_per_batch = rows_per_batch // tm

    def ada_spec(k):
        return pl.BlockSpec((None, None, None, 1, D),
                            lambda i, j: (l, i // tiles_per_batch, k, 0, 0))

    ln_spec = pl.BlockSpec((None, None, 1, D), lambda i, j: (l, k_ln, 0, 0))
    return pl.pallas_call(
        functools.partial(_ffn_kernel, alpha=alpha),
        grid=(M // tm, nf),
        in_specs=[
            pl.BlockSpec((tm, D), lambda i, j: (i, 0)),
            ada_spec(k_shift), ada_spec(k_shift + 1), ada_spec(k_shift + 2),
            pl.BlockSpec((None, D, tf), lambda i, j: (l, 0, j)),
            pl.BlockSpec((None, D, tf), lambda i, j: (l, 0, j + nf)),
            pl.BlockSpec((None, tf, D), lambda i, j: (l, j, 0)),
            ln_spec, ln_spec,
        ],
        out_specs=pl.BlockSpec((tm, D), lambda i, j: (i, 0)),
        out_shape=jax.ShapeDtypeStruct((M, D), F32),
        compiler_params=_params("arbitrary", "arbitrary"),
        name="ffn",
    )(x, ada, ada, ada, wi, wi, wo, ln_g, ln_b)


def _inproj_kernel(x_ref, sh_ref, sc_ref, cos_ref, sinlo_ref, sinhi_ref,
                   wcq_ref, wckv_ref, wkr_ref, wsb_ref, qng_ref, kvng_ref,
                   wuqn_ref, wuqr_ref, wuk_ref, wuv_ref,
                   q_ref, k_ref, v_ref, sq_ref, sk_ref, sv_ref):
    h = (x_ref[...] * (1.0 + sc_ref[...]) + sh_ref[...]).astype(BF16)
    cos, sin_lo, sin_hi = cos_ref[...], sinlo_ref[...], sinhi_ref[...]
    half = QK_ROPE_DIM // 2

    def rope(xr):
        return (xr * cos + pltpu.roll(xr, LANES - half, axis=1) * sin_lo
                + pltpu.roll(xr, half, axis=1) * sin_hi)

    c_q = jnp.dot(h, wcq_ref[...], preferred_element_type=F32)
    c_kv = jnp.dot(h, wckv_ref[...], preferred_element_type=F32)
    k_rope = jnp.dot(h, wkr_ref[...], preferred_element_type=F32)

    sb_w = SB_HEADS * HEAD_DIM
    sb = jnp.dot(h, wsb_ref[...], preferred_element_type=F32)
    sq_ref[...] = (sb[:, :sb_w] * (LOG2_E * HEAD_DIM ** -0.5)).astype(BF16)
    sk_ref[...] = sb[:, sb_w:2 * sb_w].astype(BF16)
    sv_ref[...] = sb[:, 2 * sb_w:].astype(BF16)

    c_qn = _rms_norm(c_q, qng_ref[...]).astype(BF16)
    c_kvn = _rms_norm(c_kv, kvng_ref[...]).astype(BF16)
    q_nope = jnp.dot(c_qn, wuqn_ref[...], preferred_element_type=F32)
    q_rope = jnp.dot(c_qn, wuqr_ref[...], preferred_element_type=F32)
    k_nope = jnp.dot(c_kvn, wuk_ref[...], preferred_element_type=F32)
    v_ref[...] = jnp.dot(c_kvn, wuv_ref[...], preferred_element_type=F32).astype(BF16)
    q_scale = LOG2_E * MLA_QK_DIM ** -0.5
    for hd in range(MLA_HEADS):
        lo = hd * MLA_QK_PAD
        q_ref[:, lo:lo + LANES] = (q_nope[:, hd * LANES:(hd + 1) * LANES] * q_scale).astype(BF16)
        q_ref[:, lo + LANES:lo + 2 * LANES] = (
            rope(q_rope[:, hd * LANES:(hd + 1) * LANES]) * q_scale).astype(BF16)

    k_pe = rope(k_rope).astype(BF16)
    for hd in range(MLA_HEADS):
        lo = hd * MLA_QK_PAD
        k_ref[:, lo:lo + LANES] = k_nope[:, hd * LANES:(hd + 1) * LANES].astype(BF16)
        k_ref[:, lo + LANES:lo + 2 * LANES] = k_pe


def _inproj_block(x, ada, l, k_shift, tables, w, *, rows_per_batch, tm=256):
    M, D = x.shape
    tm = min(tm, rows_per_batch)
    tiles_per_batch = rows_per_batch // tm

    def ada_spec(k):
        return pl.BlockSpec((None, None, None, 1, D),
                            lambda i: (l, i // tiles_per_batch, k, 0, 0))

    def rows(width):
        return pl.BlockSpec((tm, width), lambda i: (i, 0))

    weights = [w["cq"], w["ckv"], w["kr"], w["sb"], w["qn_g"], w["kvn_g"],
               w["uq_nope"], w["uq_rope"], w["uk"], w["uv"]]
    widths = [MLA_HEADS * MLA_QK_PAD, MLA_HEADS * MLA_QK_PAD, MLA_HEADS * HEAD_DIM,
              SB_HEADS * HEAD_DIM, SB_HEADS * HEAD_DIM, SB_HEADS * HEAD_DIM]
    return pl.pallas_call(
        _inproj_kernel,
        grid=(M // tm,),
        in_specs=[rows(D), ada_spec(k_shift), ada_spec(k_shift + 1),
                  rows(LANES), rows(LANES), rows(LANES)]
                 + [_layer_resident(a, l) for a in weights],
        out_specs=[rows(n) for n in widths],
        out_shape=[jax.ShapeDtypeStruct((M, n), BF16) for n in widths],
        compiler_params=_params("arbitrary"),
        name="inproj",
    )(x, ada, ada, *tables, *weights)


def _mla_attn_kernel(q_ref, k_ref, v_ref, o_ref, *, tk):
    qi = pl.program_id(2)
    tq = q_ref.shape[0]
    heads = q_ref.shape[1] // MLA_QK_PAD

    def step(c, carries, mask):
        kv_rows = pl.ds(pl.multiple_of(c * tk, tk), tk)
        scores = [lax.dot_general(q_ref[:, hd * MLA_QK_PAD:(hd + 1) * MLA_QK_PAD],
                                  k_ref[kv_rows, hd * MLA_QK_PAD:(hd + 1) * MLA_QK_PAD],
                                  (((1,), (1,)), ((), ())), preferred_element_type=F32)
                  for hd in range(heads)]
        stats, probs = [], []
        for s, (m, l, _) in zip(scores, carries):
            if mask is not None:
                s = jnp.where(mask, s, -jnp.inf)
            m_new = jnp.maximum(m, jnp.max(s, axis=-1, keepdims=True))
            corr = jnp.exp2(m - m_new)
            p = jnp.exp2(s - m_new)
            stats.append((m_new, l * corr + jnp.sum(p, axis=-1, keepdims=True), corr))
            probs.append(p.astype(BF16))
        out = []
        for hd, (p, (m_new, l_new, corr), (_, _, acc)) in enumerate(zip(probs, stats, carries)):
            pv = jnp.dot(p, v_ref[kv_rows, hd * HEAD_DIM:(hd + 1) * HEAD_DIM],
                         preferred_element_type=F32)
            out.append((m_new, l_new, acc * corr + pv))
        return tuple(out)

    blocks_per_q = tq // tk
    row = lax.broadcasted_iota(jnp.int32, (tq, tk), 0)
    col = lax.broadcasted_iota(jnp.int32, (tq, tk), 1)
    carries = tuple((jnp.full((tq, 1), -jnp.inf, F32), jnp.zeros((tq, 1), F32),
                     jnp.zeros((tq, HEAD_DIM), F32)) for _ in range(heads))
    for d in range(blocks_per_q):
        carries = step(qi * blocks_per_q + d, carries, col + d * tk <= row)
    carries = lax.fori_loop(0, qi * blocks_per_q, lambda c, cs: step(c, cs, None), carries)
    for hd, (_, l, acc) in enumerate(carries):
        o_ref[:, hd * HEAD_DIM:(hd + 1) * HEAD_DIM] = acc / l


def _mla_attention(q, k, v, *, batch, seq, t=512, tk=512, heads_per_step=2):
    M = q.shape[0]
    t = min(t, seq)
    tk = min(tk, t)
    nq = seq // t
    k3 = k.reshape(batch, seq, k.shape[1])
    v3 = v.reshape(batch, seq, v.shape[1])
    qk_width = heads_per_step * MLA_QK_PAD
    v_width = heads_per_step * HEAD_DIM
    return pl.pallas_call(
        functools.partial(_mla_attn_kernel, tk=tk),
        grid=(batch, MLA_HEADS // heads_per_step, nq),
        in_specs=[
            pl.BlockSpec((t, qk_width), lambda b, h, i: (b * nq + i, h)),
            pl.BlockSpec((None, seq, qk_width), lambda b, h, i: (b, 0, h)),
            pl.BlockSpec((None, seq, v_width), lambda b, h, i: (b, 0, h)),
        ],
        out_specs=pl.BlockSpec((t, v_width), lambda b, h, i: (b * nq + i, h)),
        out_shape=jax.ShapeDtypeStruct((M, MLA_HEADS * HEAD_DIM), F32),
        compiler_params=_params("arbitrary", "arbitrary", "arbitrary"),
        name="mla_attn",
    )(q, k3, v3)


def _sb_attn_kernel(q_ref, k_ref, v_ref, o_ref, *, tk, rq):
    qi = pl.program_id(2)
    tq = q_ref.shape[0]
    heads = q_ref.shape[1] // HEAD_DIM
    blocks_per_step = tq // tk
    chunks = [(hd, r) for hd in range(heads) for r in range(tq // rq)]
    jj = lax.broadcasted_iota(jnp.int32, (tk, tk), 0)
    ss = lax.broadcasted_iota(jnp.int32, (tk, tk), 1)
    suffix = (jj > ss).astype(BF16)
    sign_bit = jnp.int32(-2 ** 31)

    def step(first, carries, masked):
        row = lax.broadcasted_iota(jnp.int32, (rq, tk), 0)
        col = lax.broadcasted_iota(jnp.int32, (rq, tk), 1)
        tiles = []
        for d in reversed(range(blocks_per_step)):
            for u, (_, r) in enumerate(chunks):
                if masked and d * tk >= (r + 1) * rq - 1:
                    continue
                partial = masked and (d + 1) * tk - 1 >= r * rq
                tiles.append((u, d, (col + d * tk < row + r * rq) if partial else None))

        def kv_rows(d):
            return pl.ds(pl.multiple_of((first + d) * tk, tk), tk)

        def cols(u):
            hd = chunks[u][0]
            return slice(hd * HEAD_DIM, (hd + 1) * HEAD_DIM)

        runs = [run for run, _ in carries]
        accs = [acc for _, acc in carries]
        n = len(tiles)
        zs, log_betas, tails = [None] * n, [None] * n, [None] * n

        def scores(i):
            u, d, _ = tiles[i]
            r = chunks[u][1]
            zs[i] = lax.dot_general(q_ref[r * rq:(r + 1) * rq, cols(u)],
                                    k_ref[kv_rows(d), cols(u)],
                                    (((1,), (1,)), ((), ())), preferred_element_type=F32)

        def log_terms(i):
            u, d, mask = tiles[i]
            z = zs[i]
            neg_abs = pltpu.bitcast(pltpu.bitcast(z, jnp.int32) | sign_bit, F32)
            softplus = jnp.log(1.0 + jnp.exp2(neg_abs)) * LOG2_E
            log_beta = jnp.minimum(z, 0.0) - softplus
            log_om = log_beta - z
            if mask is not None:
                log_om = jnp.where(mask, log_om, 0.0)
            log_betas[i] = log_beta + runs[u]
            runs[u] = runs[u] + jnp.sum(log_om, axis=-1, keepdims=True)
            tails[i] = jnp.dot(log_om.astype(BF16), suffix, preferred_element_type=F32)

        def accumulate(i):
            u, d, mask = tiles[i]
            a = jnp.exp2(log_betas[i] + tails[i])
            if mask is not None:
                a = jnp.where(mask, a, 0.0)
            accs[u] = accs[u] + jnp.dot(a.astype(BF16), v_ref[kv_rows(d), cols(u)],
                                        preferred_element_type=F32)

        group, lag = 2, 1
        groups = [range(s, min(s + group, n)) for s in range(0, n, group)]
        for t in range(len(groups) + 1 + lag):
            for stage, back in ((scores, 0), (log_terms, 1), (accumulate, 1 + lag)):
                if 0 <= t - back < len(groups):
                    for i in groups[t - back]:
                        stage(i)
        return tuple(zip(runs, accs))

    def live(carries):
        top = functools.reduce(jnp.maximum, [run for run, _ in carries])
        return jnp.max(top) > SB_DEAD_LOG2

    init = tuple((jnp.zeros((rq, 1), F32), jnp.zeros((rq, HEAD_DIM), F32)) for _ in chunks)
    carries = step(qi * blocks_per_step, init, True)

    def body(state):
        n, _, carries = state
        carries = step((qi - 1 - n) * blocks_per_step, carries, False)
        return n + 1, live(carries), carries

    _, _, carries = lax.while_loop(lambda state: (state[0] < qi) & state[1], body,
                                   (jnp.int32(0), live(carries), carries))
    for (hd, r), (_, acc) in zip(chunks, carries):
        o_ref[r * rq:(r + 1) * rq, hd * HEAD_DIM:(hd + 1) * HEAD_DIM] = acc


def _sb_attention(q, k, v, *, batch, seq, tq=256, tk=256, rq=128, heads_per_step=8):
    M = q.shape[0]
    t = min(tq, seq)
    tk = min(tk, t)
    rq = min(rq, t)
    nq = seq // t
    width = heads_per_step * HEAD_DIM
    k3 = k.reshape(batch, seq, k.shape[1])
    v3 = v.reshape(batch, seq, v.shape[1])
    return pl.pallas_call(
        functools.partial(_sb_attn_kernel, tk=tk, rq=rq),
        grid=(batch, SB_HEADS // heads_per_step, nq),
        in_specs=[
            pl.BlockSpec((t, width), lambda b, h, i: (b * nq + i, h)),
            pl.BlockSpec((None, seq, width), lambda b, h, i: (b, 0, h)),
            pl.BlockSpec((None, seq, width), lambda b, h, i: (b, 0, h)),
        ],
        out_specs=pl.BlockSpec((t, width), lambda b, h, i: (b * nq + i, h)),
        out_shape=jax.ShapeDtypeStruct((M, SB_HEADS * HEAD_DIM), F32),
        compiler_params=_params("arbitrary", "arbitrary", "arbitrary"),
        name="sb_attn",
    )(q, k3, v3)


def _outproj_kernel(x_ref, om_ref, os_ref, gt_ref, mg_ref, sg_ref, wom_ref, wos_ref,
                    lng_ref, lnb_ref, o_ref, *, alpha):
    rows_total = x_ref.shape[0]
    chunk = min(OUTPROJ_ROW_CHUNK, rows_total)
    gate = (1.0 + gt_ref[...]) / alpha
    for r0 in range(0, rows_total, chunk):
        rows = slice(r0, r0 + chunk)
        om = _rms_norm(om_ref[rows, :], mg_ref[...]).astype(BF16)
        os_ = _rms_norm(os_ref[rows, :], sg_ref[...]).astype(BF16)
        m = (jnp.dot(om, wom_ref[...], preferred_element_type=F32)
             + jnp.dot(os_, wos_ref[...], preferred_element_type=F32))
        y = x_ref[rows, :] + gate * m
        o_ref[rows, :] = _layer_norm(y, lng_ref[...], lnb_ref[...], LN_EPS / (alpha * alpha))


def _outproj_block(x, o_mla, o_sb, ada, l, k_gate, w, ln_g, ln_b, k_ln, *, alpha,
                   rows_per_batch, tm=512):
    M, D = x.shape
    tm = min(tm, rows_per_batch)
    tiles_per_batch = rows_per_batch // tm

    def rows(width):
        return pl.BlockSpec((tm, width), lambda i: (i, 0))

    consts = [w["mla_out_g"], w["sb_out_g"], w["o_mla"], w["o_sb"]]
    ln_spec = pl.BlockSpec((None, None, 1, D), lambda i: (l, k_ln, 0, 0))
    return pl.pallas_call(
        functools.partial(_outproj_kernel, alpha=alpha),
        grid=(M // tm,),
        in_specs=[rows(D), rows(o_mla.shape[1]), rows(o_sb.shape[1]),
                  pl.BlockSpec((None, None, None, 1, D),
                               lambda i: (l, i // tiles_per_batch, k_gate, 0, 0))]
                 + [_layer_resident(a, l) for a in consts] + [ln_spec, ln_spec],
        out_specs=rows(D),
        out_shape=jax.ShapeDtypeStruct((M, D), F32),
        compiler_params=_params("arbitrary"),
        name="outproj",
    )(x, o_mla, o_sb, ada, *consts, ln_g, ln_b)


def _mixer_weights(w_in, q_norm_g, kv_norm_g, w_uq, w_ukv, mla_out_g, sb_out_g, w_o):
    depth = w_in.shape[0]
    o1 = Q_LORA_RANK
    o2 = o1 + KV_LORA_RANK
    o3 = o2 + QK_ROPE_DIM
    lane_pad = ((0, 0),) * 2 + ((0, LANES - QK_ROPE_DIM),)
    uq = w_uq.astype(BF16).reshape(depth, Q_LORA_RANK, MLA_HEADS, MLA_QK_DIM)
    uq_rope = jnp.pad(uq[..., QK_NOPE_DIM:], ((0, 0),) + lane_pad)
    ukv = w_ukv.astype(BF16).reshape(depth, KV_LORA_RANK, MLA_HEADS, QK_NOPE_DIM + HEAD_DIM)
    mla_w = MLA_HEADS * HEAD_DIM
    return {
        "cq": w_in[:, :, :o1].astype(BF16),
        "ckv": w_in[:, :, o1:o2].astype(BF16),
        "kr": jnp.pad(w_in[:, :, o2:o3].astype(BF16), lane_pad),
        "sb": w_in[:, :, o3:].astype(BF16),
        "qn_g": q_norm_g.reshape(depth, 1, -1),
        "kvn_g": kv_norm_g.reshape(depth, 1, -1),
        "uq_nope": uq[..., :QK_NOPE_DIM].reshape(depth, Q_LORA_RANK, MLA_HEADS * QK_NOPE_DIM),
        "uq_rope": uq_rope.reshape(depth, Q_LORA_RANK, MLA_HEADS * LANES),
        "uk": ukv[..., :QK_NOPE_DIM].reshape(depth, KV_LORA_RANK, MLA_HEADS * QK_NOPE_DIM),
        "uv": ukv[..., QK_NOPE_DIM:].reshape(depth, KV_LORA_RANK, MLA_HEADS * HEAD_DIM),
        "mla_out_g": mla_out_g.reshape(depth, 1, -1),
        "sb_out_g": sb_out_g.reshape(depth, 1, -1),
        "o_mla": w_o[:, :mla_w].astype(BF16),
        "o_sb": w_o[:, mla_w:].astype(BF16),
    }


def kernel(x, c, positions, ada_w, ada_b, ln_g, ln_b, ffn1_wi, ffn1_wo, w_in, q_norm_g, kv_norm_g, w_uq, w_ukv, mla_out_g, sb_out_g, w_o, ffn2_wi, ffn2_wo):
    B, S, D = x.shape
    depth = ada_w.shape[0]
    alpha = (2.0 * depth) ** 0.25
    M = B * S

    ada = _ada_all_layers(c, ada_w, ada_b).reshape(depth, B, N_ADA, 1, D)
    tables = _rope_tables(positions)
    ln_g = ln_g.reshape(depth, 3, 1, D)
    ln_b = ln_b.reshape(depth, 3, 1, D)
    ffn1 = (ffn1_wi.astype(BF16), ffn1_wo.astype(BF16))
    ffn2 = (ffn2_wi.astype(BF16), ffn2_wo.astype(BF16))
    w = _mixer_weights(w_in, q_norm_g, kv_norm_g, w_uq, w_ukv, mla_out_g, sb_out_g, w_o)
    xf = x.reshape(M, D)
    for l in range(depth):
        xf = _ffn_block(xf, ada, l, 0, *ffn1, ln_g, ln_b, 0, alpha=alpha, rows_per_batch=S)
        q, k, v, sq, sk, sv = _inproj_block(xf, ada, l, 3, tables, w, rows_per_batch=S)
        o_mla = _mla_attention(q, k, v, batch=B, seq=S)
        o_sb = _sb_attention(sq, sk, sv, batch=B, seq=S)
        xf = _outproj_block(xf, o_mla, o_sb, ada, l, 5, w, ln_g, ln_b, 1,
                            alpha=alpha, rows_per_batch=S)
        xf = _ffn_block(xf, ada, l, 6, *ffn2, ln_g, ln_b, 2, alpha=alpha, rows_per_batch=S)
    return xf.reshape(B, S, D)
```

```python
import functools

import jax
import jax.numpy as jnp
from jax import lax
from jax.experimental import pallas as pl
from jax.experimental.pallas import tpu as pltpu

HEAD_DIM = 128
MLA_HEADS = 8
SB_HEADS = 8
Q_LORA_RANK = 768
KV_LORA_RANK = 512
QK_NOPE_DIM = 128
QK_ROPE_DIM = 64
MLA_QK_DIM = QK_NOPE_DIM + QK_ROPE_DIM
ROPE_THETA = 10000.0
N_ADA = 9
LN_EPS = 1e-5
RMS_EPS = 1e-6
FFN_RESIDUAL_WEIGHT = 0.5
LOG2_E = 1.4426950408889634
SB_DEAD_LOG2 = -152.0

LANES = 128
MLA_QK_PAD = 2 * LANES
VMEM_LIMIT_BYTES = 60 * 1024 * 1024
OUTPROJ_ROW_CHUNK = 256

F32 = jnp.float32
BF16 = jnp.bfloat16


def _params(*semantics):
    return pltpu.CompilerParams(dimension_semantics=semantics,
                                vmem_limit_bytes=VMEM_LIMIT_BYTES)


def _layer_resident(stacked, l):
    index = (l,) + (0,) * (stacked.ndim - 1)
    return pl.BlockSpec((None,) + stacked.shape[1:], lambda *_: index,
                        pipeline_mode=pl.Buffered(1))


def _layer_norm(y, g, b, eps=LN_EPS):
    mu = jnp.mean(y, axis=-1, keepdims=True)
    d = y - mu
    var = jnp.mean(d * d, axis=-1, keepdims=True)
    return d * lax.rsqrt(var + eps) * g + b


def _rms_norm(y, g):
    return y * lax.rsqrt(jnp.mean(y * y, axis=-1, keepdims=True) + RMS_EPS) * g


def _ada_kernel(c_ref, w_ref, b_ref, o_ref):
    c = c_ref[...]
    c_act = (c * jax.nn.sigmoid(c)).astype(BF16)
    acc = jnp.dot(c_act, w_ref[...].astype(BF16), preferred_element_type=F32)
    o_ref[...] = acc + b_ref[...]


def _ada_all_layers(c, ada_w, ada_b, tn=1024):
    L, D, N = ada_w.shape
    B = c.shape[0]
    tn = min(tn, D)
    assert D % tn == 0
    return pl.pallas_call(
        _ada_kernel,
        grid=(L, N // tn),
        in_specs=[
            pl.BlockSpec((B, D), lambda l, j: (0, 0)),
            pl.BlockSpec((None, D, tn), lambda l, j: (l, 0, j)),
            pl.BlockSpec((None, 1, tn), lambda l, j: (l, 0, j)),
        ],
        out_specs=pl.BlockSpec((None, B, tn), lambda l, j: (l, 0, j)),
        out_shape=jax.ShapeDtypeStruct((L, B, N), F32),
        compiler_params=_params("arbitrary", "arbitrary"),
        name="ada",
    )(c, ada_w, ada_b.reshape(L, 1, N))


def _rope_table_kernel(pos_ref, invf_ref, cos_ref, sinlo_ref, sinhi_ref):
    half = QK_ROPE_DIM // 2
    ang = pos_ref[...].astype(F32) * invf_ref[...]
    lane = lax.broadcasted_iota(jnp.int32, ang.shape, 1)
    cos = jnp.cos(ang)
    sin = jnp.sin(ang)
    cos_ref[...] = jnp.where(lane < QK_ROPE_DIM, cos, 0.0)
    sinlo_ref[...] = jnp.where(lane < half, -sin, 0.0)
    sinhi_ref[...] = jnp.where((lane >= half) & (lane < QK_ROPE_DIM), sin, 0.0)


def _rope_tables(positions, tm=1024):
    M = positions.size
    half = QK_ROPE_DIM // 2
    inv_freq = ROPE_THETA ** (-jnp.arange(half, dtype=F32) / half)
    invf = jnp.tile(inv_freq, LANES // half).reshape(1, LANES)
    tm = min(tm, M)
    spec = pl.BlockSpec((tm, LANES), lambda i: (i, 0))
    shape = jax.ShapeDtypeStruct((M, LANES), F32)
    return pl.pallas_call(
        _rope_table_kernel,
        grid=(M // tm,),
        in_specs=[pl.BlockSpec((tm, 1), lambda i: (i, 0)),
                  pl.BlockSpec((1, LANES), lambda i: (0, 0))],
        out_specs=[spec, spec, spec],
        out_shape=[shape, shape, shape],
        compiler_params=_params("arbitrary"),
        name="rope_tables",
    )(positions.reshape(M, 1), invf)


def _ffn_kernel(x_ref, sh_ref, sc_ref, gt_ref, wg_ref, wu_ref, wo_ref, lng_ref, lnb_ref,
                o_ref, *, alpha):
    j = pl.program_id(1)

    @pl.when(j == 0)
    def _():
        o_ref[...] = jnp.zeros_like(o_ref)

    h = (x_ref[...] * (1.0 + sc_ref[...]) + sh_ref[...]).astype(BF16)
    g = jnp.dot(h, wg_ref[...], preferred_element_type=F32)
    u = jnp.dot(h, wu_ref[...], preferred_element_type=F32)
    a = (g * jax.nn.sigmoid(g) * u).astype(BF16)
    o_ref[...] += jnp.dot(a, wo_ref[...], preferred_element_type=F32)

    @pl.when(j == pl.num_programs(1) - 1)
    def _():
        c = (FFN_RESIDUAL_WEIGHT / alpha) * (1.0 + gt_ref[...])
        y = x_ref[...] + c * o_ref[...]
        o_ref[...] = _layer_norm(y, lng_ref[...], lnb_ref[...], LN_EPS / (alpha * alpha))


def _ffn_block(x, ada, l, k_shift, wi, wo, ln_g, ln_b, k_ln, *, alpha, rows_per_batch,
               tm=1024, tf=512):
    M, D = x.shape
    F = wo.shape[1]
    tm = min(tm, rows_per_batch)
    tf = min(tf, F)
    nf = F // tf
    tiles_per_batch = rows_per_batch // tm

    def ada_spec(k):
        return pl.BlockSpec((None, None, None, 1, D),
                            lambda i, j: (l, i // tiles_per_batch, k, 0, 0))

    ln_spec = pl.BlockSpec((None, None, 1, D), lambda i, j: (l, k_ln, 0, 0))
    return pl.pallas_call(
        functools.partial(_ffn_kernel, alpha=alpha),
        grid=(M // tm, nf),
        in_specs=[
            pl.BlockSpec((tm, D), lambda i, j: (i, 0)),
            ada_spec(k_shift), ada_spec(k_shift + 1), ada_spec(k_shift + 2),
            pl.BlockSpec((None, D, tf), lambda i, j: (l, 0, j)),
            pl.BlockSpec((None, D, tf), lambda i, j: (l, 0, j + nf)),
            pl.BlockSpec((None, tf, D), lambda i, j: (l, j, 0)),
            ln_spec, ln_spec,
        ],
        out_specs=pl.BlockSpec((tm, D), lambda i, j: (i, 0)),
        out_shape=jax.ShapeDtypeStruct((M, D), F32),
        compiler_params=_params("arbitrary", "arbitrary"),
        name="ffn",
    )(x, ada, ada, ada, wi, wi, wo, ln_g, ln_b)


def _inproj_kernel(x_ref, sh_ref, sc_ref, cos_ref, sinlo_ref, sinhi_ref,
                   wcq_ref, wckv_ref, wkr_ref, wsb_ref, qng_ref, kvng_ref,
                   wuqn_ref, wuqr_ref, wuk_ref, wuv_ref,
                   q_ref, k_ref, v_ref, sq_ref, sk_ref, sv_ref):
    h = (x_ref[...] * (1.0 + sc_ref[...]) + sh_ref[...]).astype(BF16)
    cos, sin_lo, sin_hi = cos_ref[...], sinlo_ref[...], sinhi_ref[...]
    half = QK_ROPE_DIM // 2

    def rope(xr):
        return (xr * cos + pltpu.roll(xr, LANES - half, axis=1) * sin_lo
                + pltpu.roll(xr, half, axis=1) * sin_hi)

    c_q = jnp.dot(h, wcq_ref[...], preferred_element_type=F32)
    c_kv = jnp.dot(h, wckv_ref[...], preferred_element_type=F32)
    k_rope = jnp.dot(h, wkr_ref[...], preferred_element_type=F32)

    sb_w = SB_HEADS * HEAD_DIM
    sb = jnp.dot(h, wsb_ref[...], preferred_element_type=F32)
    sq_ref[...] = (sb[:, :sb_w] * (LOG2_E * HEAD_DIM ** -0.5)).astype(BF16)
    sk_ref[...] = sb[:, sb_w:2 * sb_w].astype(BF16)
    sv_ref[...] = sb[:, 2 * sb_w:].astype(BF16)

    c_qn = _rms_norm(c_q, qng_ref[...]).astype(BF16)
    c_kvn = _rms_norm(c_kv, kvng_ref[...]).astype(BF16)
    q_nope = jnp.dot(c_qn, wuqn_ref[...], preferred_element_type=F32)
    q_rope = jnp.dot(c_qn, wuqr_ref[...], preferred_element_type=F32)
    k_nope = jnp.dot(c_kvn, wuk_ref[...], preferred_element_type=F32)
    v_ref[...] = jnp.dot(c_kvn, wuv_ref[...], preferred_element_type=F32).astype(BF16)
    q_scale = LOG2_E * MLA_QK_DIM ** -0.5
    for hd in range(MLA_HEADS):
        lo = hd * MLA_QK_PAD
        q_ref[:, lo:lo + LANES] = (q_nope[:, hd * LANES:(hd + 1) * LANES] * q_scale).astype(BF16)
        q_ref[:, lo + LANES:lo + 2 * LANES] = (
            rope(q_rope[:, hd * LANES:(hd + 1) * LANES]) * q_scale).astype(BF16)

    k_pe = rope(k_rope).astype(BF16)
    for hd in range(MLA_HEADS):
        lo = hd * MLA_QK_PAD
        k_ref[:, lo:lo + LANES] = k_nope[:, hd * LANES:(hd + 1) * LANES].astype(BF16)
        k_ref[:, lo + LANES:lo + 2 * LANES] = k_pe


def _inproj_block(x, ada, l, k_shift, tables, w, *, rows_per_batch, tm=256):
    M, D = x.shape
    tm = min(tm, rows_per_batch)
    tiles_per_batch = rows_per_batch // tm

    def ada_spec(k):
        return pl.BlockSpec((None, None, None, 1, D),
                            lambda i: (l, i // tiles_per_batch, k, 0, 0))

    def rows(width):
        return pl.BlockSpec((tm, width), lambda i: (i, 0))

    weights = [w["cq"], w["ckv"], w["kr"], w["sb"], w["qn_g"], w["kvn_g"],
               w["uq_nope"], w["uq_rope"], w["uk"], w["uv"]]
    widths = [MLA_HEADS * MLA_QK_PAD, MLA_HEADS * MLA_QK_PAD, MLA_HEADS * HEAD_DIM,
              SB_HEADS * HEAD_DIM, SB_HEADS * HEAD_DIM, SB_HEADS * HEAD_DIM]
    return pl.pallas_call(
        _inproj_kernel,
        grid=(M // tm,),
        in_specs=[rows(D), ada_spec(k_shift), ada_spec(k_shift + 1),
                  rows(LANES), rows(LANES), rows(LANES)]
                 + [_layer_resident(a, l) for a in weights],
        out_specs=[rows(n) for n in widths],
        out_shape=[jax.ShapeDtypeStruct((M, n), BF16) for n in widths],
        compiler_params=_params("arbitrary"),
        name="inproj",
    )(x, ada, ada, *tables, *weights)


def _mla_attn_kernel(q_ref, k_ref, v_ref, o_ref, m_ref, l_ref, acc_ref):
    qi = pl.program_id(2)
    tq = tk = q_ref.shape[0]
    heads = q_ref.shape[1] // MLA_QK_PAD
    reps = tk // LANES

    def step(c, first):
        kv_rows = pl.ds(pl.multiple_of(c * tk, tk), tk)
        scores = [lax.dot_general(q_ref[:, hd * MLA_QK_PAD:(hd + 1) * MLA_QK_PAD],
                                  k_ref[kv_rows, hd * MLA_QK_PAD:(hd + 1) * MLA_QK_PAD],
                                  (((1,), (1,)), ((), ())), preferred_element_type=F32)
                  for hd in range(heads)]
        probs, corrs = [], []
        for hd, s in enumerate(scores):
            if first:
                row = lax.broadcasted_iota(jnp.int32, (tq, tk), 0)
                col = lax.broadcasted_iota(jnp.int32, (tq, tk), 1)
                s = jnp.where(col <= row, s, -jnp.inf)
                m_new = jnp.broadcast_to(jnp.max(s, axis=-1, keepdims=True), (tq, LANES))
                p = jnp.exp2(s - jnp.tile(m_new, (1, reps)))
                l_ref[hd] = jnp.broadcast_to(jnp.sum(p, axis=-1, keepdims=True), (tq, LANES))
                corrs.append(None)
            else:
                m_prev = m_ref[hd]
                m_new = jnp.maximum(m_prev, jnp.max(s, axis=-1, keepdims=True))
                corr = jnp.exp2(m_prev - m_new)
                p = jnp.exp2(s - jnp.tile(m_new, (1, reps)))
                l_ref[hd] = l_ref[hd] * corr + jnp.sum(p, axis=-1, keepdims=True)
                corrs.append(corr)
            m_ref[hd] = m_new
            probs.append(p.astype(BF16))
        for hd, (p, corr) in enumerate(zip(probs, corrs)):
            pv = jnp.dot(p, v_ref[kv_rows, hd * HEAD_DIM:(hd + 1) * HEAD_DIM],
                         preferred_element_type=F32)
            acc_ref[hd] = pv if first else acc_ref[hd] * corr + pv

    step(qi, True)

    def body(c, carry):
        step(c, False)
        return carry

    lax.fori_loop(0, qi, body, 0)
    for hd in range(heads):
        o_ref[:, hd * HEAD_DIM:(hd + 1) * HEAD_DIM] = acc_ref[hd] / l_ref[hd]


def _mla_attention(q, k, v, *, batch, seq, t=512, heads_per_step=4):
    M = q.shape[0]
    t = min(t, seq)
    nq = seq // t
    k3 = k.reshape(batch, seq, k.shape[1])
    v3 = v.reshape(batch, seq, v.shape[1])
    qk_width = heads_per_step * MLA_QK_PAD
    v_width = heads_per_step * HEAD_DIM
    return pl.pallas_call(
        _mla_attn_kernel,
        scratch_shapes=[pltpu.VMEM((heads_per_step, t, LANES), F32),
                        pltpu.VMEM((heads_per_step, t, LANES), F32),
                        pltpu.VMEM((heads_per_step, t, HEAD_DIM), F32)],
        grid=(batch, MLA_HEADS // heads_per_step, nq),
        in_specs=[
            pl.BlockSpec((t, qk_width), lambda b, h, i: (b * nq + i, h)),
            pl.BlockSpec((None, seq, qk_width), lambda b, h, i: (b, 0, h)),
            pl.BlockSpec((None, seq, v_width), lambda b, h, i: (b, 0, h)),
        ],
        out_specs=pl.BlockSpec((t, v_width), lambda b, h, i: (b * nq + i, h)),
        out_shape=jax.ShapeDtypeStruct((M, MLA_HEADS * HEAD_DIM), F32),
        compiler_params=_params("arbitrary", "arbitrary", "arbitrary"),
        name="mla_attn",
    )(q, k3, v3)


def _sb_attn_kernel(q_ref, k_ref, v_ref, o_ref, run_ref, *, tk, rq):
    qi = pl.program_id(2)
    tq = q_ref.shape[0]
    heads = q_ref.shape[1] // HEAD_DIM
    blocks_per_step = tq // tk
    chunks = [(hd, r) for hd in range(heads) for r in range(tq // rq)]
    jj = lax.broadcasted_iota(jnp.int32, (tk, tk), 0)
    ss = lax.broadcasted_iota(jnp.int32, (tk, tk), 1)
    suffix = (jj > ss).astype(BF16)
    sign_bit = jnp.int32(-2 ** 31)

    def step(first, masked):
        row = lax.broadcasted_iota(jnp.int32, (rq, tk), 0)
        col = lax.broadcasted_iota(jnp.int32, (rq, tk), 1)
        tiles = []
        for d in reversed(range(blocks_per_step)):
            for u, (_, r) in enumerate(chunks):
                if masked and d * tk >= (r + 1) * rq - 1:
                    continue
                partial = masked and (d + 1) * tk - 1 >= r * rq
                tiles.append((u, d, (col + d * tk < row + r * rq) if partial else None))

        def kv_rows(d):
            return pl.ds(pl.multiple_of((first + d) * tk, tk), tk)

        def cols(u):
            hd = chunks[u][0]
            return slice(hd * HEAD_DIM, (hd + 1) * HEAD_DIM)

        def rows(u):
            r = chunks[u][1]
            return slice(r * rq, (r + 1) * rq)

        runs = [jnp.zeros((rq, LANES), F32) if masked else None for _ in chunks]
        started = [not masked for _ in chunks]
        n = len(tiles)
        zs, log_betas, tails = [None] * n, [None] * n, [None] * n

        def scores(i):
            u, d, _ = tiles[i]
            zs[i] = lax.dot_general(q_ref[rows(u), cols(u)], k_ref[kv_rows(d), cols(u)],
                                    (((1,), (1,)), ((), ())), preferred_element_type=F32)

        def log_terms(i):
            u, d, mask = tiles[i]
            z = zs[i]
            neg_abs = pltpu.bitcast(pltpu.bitcast(z, jnp.int32) | sign_bit, F32)
            softplus = jnp.log(1.0 + jnp.exp2(neg_abs)) * LOG2_E
            log_beta = jnp.minimum(z, 0.0) - softplus
            log_om = log_beta - z
            if mask is not None:
                log_om = jnp.where(mask, log_om, 0.0)
            if runs[u] is None:
                runs[u] = run_ref[u]
            log_betas[i] = log_beta + jnp.tile(runs[u], (1, tk // LANES))
            runs[u] = runs[u] + jnp.sum(log_om, axis=-1, keepdims=True)
            tails[i] = jnp.dot(log_om.astype(BF16), suffix, preferred_element_type=F32)

        def accumulate(i):
            u, d, mask = tiles[i]
            a = jnp.exp2(log_betas[i] + tails[i])
            if mask is not None:
                a = jnp.where(mask, a, 0.0)
            av = jnp.dot(a.astype(BF16), v_ref[kv_rows(d), cols(u)], preferred_element_type=F32)
            o_ref[rows(u), cols(u)] = o_ref[rows(u), cols(u)] + av if started[u] else av
            started[u] = True

        group, lag = 2, 1
        groups = [range(s, min(s + group, n)) for s in range(0, n, group)]
        for t in range(len(groups) + 1 + lag):
            for stage, back in ((scores, 0), (log_terms, 1), (accumulate, 1 + lag)):
                if 0 <= t - back < len(groups):
                    for i in groups[t - back]:
                        stage(i)
        for u in range(len(chunks)):
            if runs[u] is not None:
                run_ref[u] = runs[u]
            if not started[u]:
                o_ref[rows(u), cols(u)] = jnp.zeros((rq, HEAD_DIM), F32)
        return jnp.max(functools.reduce(
            jnp.maximum, [run_ref[u] if run is None else run for u, run in enumerate(runs)]))

    top = step(qi * blocks_per_step, True)

    def body(state):
        n, _ = state
        return n + 1, step((qi - 1 - n) * blocks_per_step, False)

    lax.while_loop(lambda state: (state[0] < qi) & (state[1] > SB_DEAD_LOG2), body,
                   (jnp.int32(0), top))


def _sb_attention(q, k, v, *, batch, seq, tq=256, tk=256, rq=128, heads_per_step=8):
    M = q.shape[0]
    t = min(tq, seq)
    tk = min(tk, t)
    rq = min(rq, t)
    nq = seq // t
    width = heads_per_step * HEAD_DIM
    k3 = k.reshape(batch, seq, k.shape[1])
    v3 = v.reshape(batch, seq, v.shape[1])
    return pl.pallas_call(
        functools.partial(_sb_attn_kernel, tk=tk, rq=rq),
        scratch_shapes=[pltpu.VMEM((heads_per_step * (t // rq), rq, LANES), F32)],
        grid=(batch, SB_HEADS // heads_per_step, nq),
        in_specs=[
            pl.BlockSpec((t, width), lambda b, h, i: (b * nq + i, h)),
            pl.BlockSpec((None, seq, width), lambda b, h, i: (b, 0, h)),
            pl.BlockSpec((None, seq, width), lambda b, h, i: (b, 0, h)),
        ],
        out_specs=pl.BlockSpec((t, width), lambda b, h, i: (b * nq + i, h)),
        out_shape=jax.ShapeDtypeStruct((M, SB_HEADS * HEAD_DIM), F32),
        compiler_params=_params("arbitrary", "arbitrary", "arbitrary"),
        name="sb_attn",
    )(q, k3, v3)


def _outproj_kernel(x_ref, om_ref, os_ref, gt_ref, mg_ref, sg_ref, wom_ref, wos_ref,
                    lng_ref, lnb_ref, o_ref, *, alpha):
    rows_total = x_ref.shape[0]
    chunk = min(OUTPROJ_ROW_CHUNK, rows_total)
    gate = (1.0 + gt_ref[...]) / alpha
    for r0 in range(0, rows_total, chunk):
        rows = slice(r0, r0 + chunk)
        om = _rms_norm(om_ref[rows, :], mg_ref[...]).astype(BF16)
        os_ = _rms_norm(os_ref[rows, :], sg_ref[...]).astype(BF16)
        m = (jnp.dot(om, wom_ref[...], preferred_element_type=F32)
             + jnp.dot(os_, wos_ref[...], preferred_element_type=F32))
        y = x_ref[rows, :] + gate * m
        o_ref[rows, :] = _layer_norm(y, lng_ref[...], lnb_ref[...], LN_EPS / (alpha * alpha))


def _outproj_block(x, o_mla, o_sb, ada, l, k_gate, w, ln_g, ln_b, k_ln, *, alpha,
                   rows_per_batch, tm=512):
    M, D = x.shape
    tm = min(tm, rows_per_batch)
    tiles_per_batch = rows_per_batch // tm

    def rows(width):
        return pl.BlockSpec((tm, width), lambda i: (i, 0))

    consts = [w["mla_out_g"], w["sb_out_g"], w["o_mla"], w["o_sb"]]
    ln_spec = pl.BlockSpec((None, None, 1, D), lambda i: (l, k_ln, 0, 0))
    return pl.pallas_call(
        functools.partial(_outproj_kernel, alpha=alpha),
        grid=(M // tm,),
        in_specs=[rows(D), rows(o_mla.shape[1]), rows(o_sb.shape[1]),
                  pl.BlockSpec((None, None, None, 1, D),
                               lambda i: (l, i // tiles_per_batch, k_gate, 0, 0))]
                 + [_layer_resident(a, l) for a in consts] + [ln_spec, ln_spec],
        out_specs=rows(D),
        out_shape=jax.ShapeDtypeStruct((M, D), F32),
        compiler_params=_params("arbitrary"),
        name="outproj",
    )(x, o_mla, o_sb, ada, *consts, ln_g, ln_b)


def _mixer_weights(w_in, q_norm_g, kv_norm_g, w_uq, w_ukv, mla_out_g, sb_out_g, w_o):
    depth = w_in.shape[0]
    o1 = Q_LORA_RANK
    o2 = o1 + KV_LORA_RANK
    o3 = o2 + QK_ROPE_DIM
    lane_pad = ((0, 0),) * 2 + ((0, LANES - QK_ROPE_DIM),)
    uq = w_uq.astype(BF16).reshape(depth, Q_LORA_RANK, MLA_HEADS, MLA_QK_DIM)
    uq_rope = jnp.pad(uq[..., QK_NOPE_DIM:], ((0, 0),) + lane_pad)
    ukv = w_ukv.astype(BF16).reshape(depth, KV_LORA_RANK, MLA_HEADS, QK_NOPE_DIM + HEAD_DIM)
    mla_w = MLA_HEADS * HEAD_DIM
    return {
        "cq": w_in[:, :, :o1].astype(BF16),
        "ckv": w_in[:, :, o1:o2].astype(BF16),
        "kr": jnp.pad(w_in[:, :, o2:o3].astype(BF16), lane_pad),
        "sb": w_in[:, :, o3:].astype(BF16),
        "qn_g": q_norm_g.reshape(depth, 1, -1),
        "kvn_g": kv_norm_g.reshape(depth, 1, -1),
        "uq_nope": uq[..., :QK_NOPE_DIM].reshape(depth, Q_LORA_RANK, MLA_HEADS * QK_NOPE_DIM),
        "uq_rope": uq_rope.reshape(depth, Q_LORA_RANK, MLA_HEADS * LANES),
        "uk": ukv[..., :QK_NOPE_DIM].reshape(depth, KV_LORA_RANK, MLA_HEADS * QK_NOPE_DIM),
        "uv": ukv[..., QK_NOPE_DIM:].reshape(depth, KV_LORA_RANK, MLA_HEADS * HEAD_DIM),
        "mla_out_g": mla_out_g.reshape(depth, 1, -1),
        "sb_out_g": sb_out_g.reshape(depth, 1, -1),
        "o_mla": w_o[:, :mla_w].astype(BF16),
        "o_sb": w_o[:, mla_w:].astype(BF16),
    }


def kernel(x, c, positions, ada_w, ada_b, ln_g, ln_b, ffn1_wi, ffn1_wo, w_in, q_norm_g, kv_norm_g, w_uq, w_ukv, mla_out_g, sb_out_g, w_o, ffn2_wi, ffn2_wo):
    B, S, D = x.shape
    depth = ada_w.shape[0]
    alpha = (2.0 * depth) ** 0.25
    M = B * S

    ada = _ada_all_layers(c, ada_w, ada_b).reshape(depth, B, N_ADA, 1, D)
    tables = _rope_tables(positions)
    ln_g = ln_g.reshape(depth, 3, 1, D)
    ln_b = ln_b.reshape(depth, 3, 1, D)
    ffn1 = (ffn1_wi.astype(BF16), ffn1_wo.astype(BF16))
    ffn2 = (ffn2_wi.astype(BF16), ffn2_wo.astype(BF16))
    w = _mixer_weights(w_in, q_norm_g, kv_norm_g, w_uq, w_ukv, mla_out_g, sb_out_g, w_o)
    xf = x.reshape(M, D)
    for l in range(depth):
        xf = _ffn_block(xf, ada, l, 0, *ffn1, ln_g, ln_b, 0, alpha=alpha, rows_per_batch=S)
        q, k, v, sq, sk, sv = _inproj_block(xf, ada, l, 3, tables, w, rows_per_batch=S)
        o_mla = _mla_attention(q, k, v, batch=B, seq=S)
        o_sb = _sb_attention(sq, sk, sv, batch=B, seq=S)
        xf = _outproj_block(xf, o_mla, o_sb, ada, l, 5, w, ln_g, ln_b, 1,
                            alpha=alpha, rows_per_batch=S)
        xf = _ffn_block(xf, ada, l, 6, *ffn2, ln_g, ln_b, 2, alpha=alpha, rows_per_batch=S)
    return xf.reshape(B, S, D)
```

```python
import functools

import jax
import jax.numpy as jnp
from jax import lax
from jax.experimental import pallas as pl
from jax.experimental.pallas import tpu as pltpu

HEAD_DIM = 128
MLA_HEADS = 8
SB_HEADS = 8
Q_LORA_RANK = 768
KV_LORA_RANK = 512
QK_NOPE_DIM = 128
QK_ROPE_DIM = 64
MLA_QK_DIM = QK_NOPE_DIM + QK_ROPE_DIM
ROPE_THETA = 10000.0
N_ADA = 9
LN_EPS = 1e-5
RMS_EPS = 1e-6
FFN_RESIDUAL_WEIGHT = 0.5
LOG2_E = 1.4426950408889634
SB_DEAD_LOG2 = -152.0

LANES = 128
MLA_QK_PAD = 2 * LANES
VMEM_LIMIT_BYTES = 60 * 1024 * 1024
OUTPROJ_ROW_CHUNK = 256

F32 = jnp.float32
BF16 = jnp.bfloat16


def _params(*semantics):
    return pltpu.CompilerParams(dimension_semantics=semantics,
                                vmem_limit_bytes=VMEM_LIMIT_BYTES)


def _layer_resident(stacked, l):
    index = (l,) + (0,) * (stacked.ndim - 1)
    return pl.BlockSpec((None,) + stacked.shape[1:], lambda *_: index,
                        pipeline_mode=pl.Buffered(1))


def _layer_norm(y, g, b, eps=LN_EPS):
    mu = jnp.mean(y, axis=-1, keepdims=True)
    d = y - mu
    var = jnp.mean(d * d, axis=-1, keepdims=True)
    return d * lax.rsqrt(var + eps) * g + b


def _rms_norm(y, g):
    return y * lax.rsqrt(jnp.mean(y * y, axis=-1, keepdims=True) + RMS_EPS) * g


def _ada_kernel(c_ref, w_ref, b_ref, o_ref):
    c = c_ref[...]
    c_act = (c * jax.nn.sigmoid(c)).astype(BF16)
    acc = jnp.dot(c_act, w_ref[...].astype(BF16), preferred_element_type=F32)
    o_ref[...] = acc + b_ref[...]


def _ada_all_layers(c, ada_w, ada_b, tn=1024):
    L, D, N = ada_w.shape
    B = c.shape[0]
    tn = min(tn, D)
    assert D % tn == 0
    return pl.pallas_call(
        _ada_kernel,
        grid=(L, N // tn),
        in_specs=[
            pl.BlockSpec((B, D), lambda l, j: (0, 0)),
            pl.BlockSpec((None, D, tn), lambda l, j: (l, 0, j)),
            pl.BlockSpec((None, 1, tn), lambda l, j: (l, 0, j)),
        ],
        out_specs=pl.BlockSpec((None, B, tn), lambda l, j: (l, 0, j)),
        out_shape=jax.ShapeDtypeStruct((L, B, N), F32),
        compiler_params=_params("arbitrary", "arbitrary"),
        name="ada",
    )(c, ada_w, ada_b.reshape(L, 1, N))


def _rope_table_kernel(pos_ref, invf_ref, cos_ref, sinlo_ref, sinhi_ref):
    half = QK_ROPE_DIM // 2
    ang = pos_ref[...].astype(F32) * invf_ref[...]
    lane = lax.broadcasted_iota(jnp.int32, ang.shape, 1)
    cos = jnp.cos(ang)
    sin = jnp.sin(ang)
    cos_ref[...] = jnp.where(lane < QK_ROPE_DIM, cos, 0.0)
    sinlo_ref[...] = jnp.where(lane < half, -sin, 0.0)
    sinhi_ref[...] = jnp.where((lane >= half) & (lane < QK_ROPE_DIM), sin, 0.0)


def _rope_tables(positions, tm=1024):
    M = positions.size
    half = QK_ROPE_DIM // 2
    inv_freq = ROPE_THETA ** (-jnp.arange(half, dtype=F32) / half)
    invf = jnp.tile(inv_freq, LANES // half).reshape(1, LANES)
    tm = min(tm, M)
    spec = pl.BlockSpec((tm, LANES), lambda i: (i, 0))
    shape = jax.ShapeDtypeStruct((M, LANES), F32)
    return pl.pallas_call(
        _rope_table_kernel,
        grid=(M // tm,),
        in_specs=[pl.BlockSpec((tm, 1), lambda i: (i, 0)),
                  pl.BlockSpec((1, LANES), lambda i: (0, 0))],
        out_specs=[spec, spec, spec],
        out_shape=[shape, shape, shape],
        compiler_params=_params("arbitrary"),
        name="rope_tables",
    )(positions.reshape(M, 1), invf)


def _ffn_kernel(x_ref, sh_ref, sc_ref, gt_ref, wg_ref, wu_ref, wo_ref, lng_ref, lnb_ref,
                o_ref, *, alpha):
    j = pl.program_id(1)

    @pl.when(j == 0)
    def _():
        o_ref[...] = jnp.zeros_like(o_ref)

    h = (x_ref[...] * (1.0 + sc_ref[...]) + sh_ref[...]).astype(BF16)
    g = jnp.dot(h, wg_ref[...], preferred_element_type=F32)
    u = jnp.dot(h, wu_ref[...], preferred_element_type=F32)
    a = (g * jax.nn.sigmoid(g) * u).astype(BF16)
    o_ref[...] += jnp.dot(a, wo_ref[...], preferred_element_type=F32)

    @pl.when(j == pl.num_programs(1) - 1)
    def _():
        c = (FFN_RESIDUAL_WEIGHT / alpha) * (1.0 + gt_ref[...])
        y = x_ref[...] + c * o_ref[...]
        o_ref[...] = _layer_norm(y, lng_ref[...], lnb_ref[...], LN_EPS / (alpha * alpha))


def _ffn_block(x, ada, l, k_shift, wi, wo, ln_g, ln_b, k_ln, *, alpha, rows_per_batch,
               tm=1024, tf=512):
    M, D = x.shape
    F = wo.shape[1]
    tm = min(tm, rows_per_batch)
    tf = min(tf, F)
    nf = F // tf
    tiles_per_batch = rows_per_batch // tm

    def ada_spec(k):
        return pl.BlockSpec((None, None, None, 1, D),
                            lambda i, j: (l, i // tiles_per_batch, k, 0, 0))

    ln_spec = pl.BlockSpec((None, None, 1, D), lambda i, j: (l, k_ln, 0, 0))
    return pl.pallas_call(
        functools.partial(_ffn_kernel, alpha=alpha),
        grid=(M // tm, nf),
        in_specs=[
            pl.BlockSpec((tm, D), lambda i, j: (i, 0)),
            ada_spec(k_shift), ada_spec(k_shift + 1), ada_spec(k_shift + 2),
            pl.BlockSpec((None, D, tf), lambda i, j: (l, 0, j)),
            pl.BlockSpec((None, D, tf), lambda i, j: (l, 0, j + nf)),
            pl.BlockSpec((None, tf, D), lambda i, j: (l, j, 0)),
            ln_spec, ln_spec,
        ],
        out_specs=pl.BlockSpec((tm, D), lambda i, j: (i, 0)),
        out_shape=jax.ShapeDtypeStruct((M, D), F32),
        compiler_params=_params("arbitrary", "arbitrary"),
        name="ffn",
    )(x, ada, ada, ada, wi, wi, wo, ln_g, ln_b)


def _inproj_kernel(x_ref, sh_ref, sc_ref, cos_ref, sinlo_ref, sinhi_ref,
                   wcq_ref, wckv_ref, wkr_ref, wsb_ref, qng_ref, kvng_ref,
                   wuqn_ref, wuqr_ref, wuk_ref, wuv_ref,
                   q_ref, k_ref, v_ref, sq_ref, sk_ref, sv_ref):
    h = (x_ref[...] * (1.0 + sc_ref[...]) + sh_ref[...]).astype(BF16)
    cos, sin_lo, sin_hi = cos_ref[...], sinlo_ref[...], sinhi_ref[...]
    half = QK_ROPE_DIM // 2

    def rope(xr):
        return (xr * cos + pltpu.roll(xr, LANES - half, axis=1) * sin_lo
                + pltpu.roll(xr, half, axis=1) * sin_hi)

    c_q = jnp.dot(h, wcq_ref[...], preferred_element_type=F32)
    c_kv = jnp.dot(h, wckv_ref[...], preferred_element_type=F32)
    k_rope = jnp.dot(h, wkr_ref[...], preferred_element_type=F32)

    sb_w = SB_HEADS * HEAD_DIM
    sb = jnp.dot(h, wsb_ref[...], preferred_element_type=F32)
    sq_ref[...] = (sb[:, :sb_w] * (LOG2_E * HEAD_DIM ** -0.5)).astype(BF16)
    sk_ref[...] = sb[:, sb_w:2 * sb_w].astype(BF16)
    sv_ref[...] = sb[:, 2 * sb_w:].astype(BF16)

    c_qn = _rms_norm(c_q, qng_ref[...]).astype(BF16)
    c_kvn = _rms_norm(c_kv, kvng_ref[...]).astype(BF16)
    q_nope = jnp.dot(c_qn, wuqn_ref[...], preferred_element_type=F32)
    q_rope = jnp.dot(c_qn, wuqr_ref[...], preferred_element_type=F32)
    k_nope = jnp.dot(c_kvn, wuk_ref[...], preferred_element_type=F32)
    v_ref[...] = jnp.dot(c_kvn, wuv_ref[...], preferred_element_type=F32).astype(BF16)
    q_scale = LOG2_E * MLA_QK_DIM ** -0.5
    for hd in range(MLA_HEADS):
        lo = hd * MLA_QK_PAD
        q_ref[:, lo:lo + LANES] = (q_nope[:, hd * LANES:(hd + 1) * LANES] * q_scale).astype(BF16)
        q_ref[:, lo + LANES:lo + 2 * LANES] = (
            rope(q_rope[:, hd * LANES:(hd + 1) * LANES]) * q_scale).astype(BF16)

    k_pe = rope(k_rope).astype(BF16)
    for hd in range(MLA_HEADS):
        lo = hd * MLA_QK_PAD
        k_ref[:, lo:lo + LANES] = k_nope[:, hd * LANES:(hd + 1) * LANES].astype(BF16)
        k_ref[:, lo + LANES:lo + 2 * LANES] = k_pe


def _inproj_block(x, ada, l, k_shift, tables, w, *, rows_per_batch, tm=256):
    M, D = x.shape
    tm = min(tm, rows_per_batch)
    tiles_per_batch = rows_per_batch // tm

    def ada_spec(k):
        return pl.BlockSpec((None, None, None, 1, D),
                            lambda i: (l, i // tiles_per_batch, k, 0, 0))

    def rows(width):
        return pl.BlockSpec((tm, width), lambda i: (i, 0))

    weights = [w["cq"], w["ckv"], w["kr"], w["sb"], w["qn_g"], w["kvn_g"],
               w["uq_nope"], w["uq_rope"], w["uk"], w["uv"]]
    widths = [MLA_HEADS * MLA_QK_PAD, MLA_HEADS * MLA_QK_PAD, MLA_HEADS * HEAD_DIM,
              SB_HEADS * HEAD_DIM, SB_HEADS * HEAD_DIM, SB_HEADS * HEAD_DIM]
    return pl.pallas_call(
        _inproj_kernel,
        grid=(M // tm,),
        in_specs=[rows(D), ada_spec(k_shift), ada_spec(k_shift + 1),
                  rows(LANES), rows(LANES), rows(LANES)]
                 + [_layer_resident(a, l) for a in weights],
        out_specs=[rows(n) for n in widths],
        out_shape=[jax.ShapeDtypeStruct((M, n), BF16) for n in widths],
        compiler_params=_params("arbitrary"),
        name="inproj",
    )(x, ada, ada, *tables, *weights)


def _mla_attn_kernel(q_ref, k_ref, v_ref, o_ref, m_ref, l_ref, acc_ref):
    tq = tk = m_ref.shape[1]
    for g in range(q_ref.shape[0] // tq):
        _mla_sweep(q_ref.at[g * tq:(g + 1) * tq], k_ref, v_ref, o_ref.at[g * tq:(g + 1) * tq],
                   m_ref, l_ref, acc_ref, pl.program_id(2) * (q_ref.shape[0] // tq) + g)


def _mla_sweep(q_ref, k_ref, v_ref, o_ref, m_ref, l_ref, acc_ref, qi):
    tq = tk = q_ref.shape[0]
    heads = q_ref.shape[1] // MLA_QK_PAD
    reps = tk // LANES

    def step(c, first):
        kv_rows = pl.ds(pl.multiple_of(c * tk, tk), tk)
        scores = [lax.dot_general(q_ref[:, hd * MLA_QK_PAD:(hd + 1) * MLA_QK_PAD],
                                  k_ref[kv_rows, hd * MLA_QK_PAD:(hd + 1) * MLA_QK_PAD],
                                  (((1,), (1,)), ((), ())), preferred_element_type=F32)
                  for hd in range(heads)]
        probs, corrs = [], []
        for hd, s in enumerate(scores):
            if first:
                row = lax.broadcasted_iota(jnp.int32, (tq, tk), 0)
                col = lax.broadcasted_iota(jnp.int32, (tq, tk), 1)
                s = jnp.where(col <= row, s, -jnp.inf)
                m_new = jnp.broadcast_to(jnp.max(s, axis=-1, keepdims=True), (tq, LANES))
                p = jnp.exp2(s - jnp.tile(m_new, (1, reps)))
                l_ref[hd] = jnp.broadcast_to(jnp.sum(p, axis=-1, keepdims=True), (tq, LANES))
                corrs.append(None)
            else:
                m_prev = m_ref[hd]
                m_new = jnp.maximum(m_prev, jnp.max(s, axis=-1, keepdims=True))
                corr = jnp.exp2(m_prev - m_new)
                p = jnp.exp2(s - jnp.tile(m_new, (1, reps)))
                l_ref[hd] = l_ref[hd] * corr + jnp.sum(p, axis=-1, keepdims=True)
                corrs.append(corr)
            m_ref[hd] = m_new
            probs.append(p.astype(BF16))
        for hd, (p, corr) in enumerate(zip(probs, corrs)):
            pv = jnp.dot(p, v_ref[kv_rows, hd * HEAD_DIM:(hd + 1) * HEAD_DIM],
                         preferred_element_type=F32)
            acc_ref[hd] = pv if first else acc_ref[hd] * corr + pv

    step(qi, True)

    def body(c, carry):
        step(c, False)
        return carry

    lax.fori_loop(0, qi, body, 0)
    for hd in range(heads):
        o_ref[:, hd * HEAD_DIM:(hd + 1) * HEAD_DIM] = acc_ref[hd] / l_ref[hd]


def _mla_attention(q, k, v, *, batch, seq, t=512, heads_per_step=4, q_blocks_per_step=4):
    M = q.shape[0]
    t = min(t, seq)
    q_blocks_per_step = min(q_blocks_per_step, seq // t)
    tq = t * q_blocks_per_step
    nq = seq // tq
    k3 = k.reshape(batch, seq, k.shape[1])
    v3 = v.reshape(batch, seq, v.shape[1])
    qk_width = heads_per_step * MLA_QK_PAD
    v_width = heads_per_step * HEAD_DIM
    return pl.pallas_call(
        _mla_attn_kernel,
        scratch_shapes=[pltpu.VMEM((heads_per_step, t, LANES), F32),
                        pltpu.VMEM((heads_per_step, t, LANES), F32),
                        pltpu.VMEM((heads_per_step, t, HEAD_DIM), F32)],
        grid=(batch, MLA_HEADS // heads_per_step, nq),
        in_specs=[
            pl.BlockSpec((tq, qk_width), lambda b, h, i: (b * nq + i, h)),
            pl.BlockSpec((None, seq, qk_width), lambda b, h, i: (b, 0, h)),
            pl.BlockSpec((None, seq, v_width), lambda b, h, i: (b, 0, h)),
        ],
        out_specs=pl.BlockSpec((tq, v_width), lambda b, h, i: (b * nq + i, h)),
        out_shape=jax.ShapeDtypeStruct((M, MLA_HEADS * HEAD_DIM), F32),
        compiler_params=_params("arbitrary", "arbitrary", "arbitrary"),
        name="mla_attn",
    )(q, k3, v3)


def _sb_attn_kernel(q_ref, k_ref, v_ref, o_ref, run_ref, *, tq, tk, rq):
    for g in range(q_ref.shape[0] // tq):
        _sb_sweep(q_ref.at[g * tq:(g + 1) * tq], k_ref, v_ref, o_ref.at[g * tq:(g + 1) * tq],
                  run_ref, pl.program_id(2) * (q_ref.shape[0] // tq) + g, tk=tk, rq=rq)


def _sb_sweep(q_ref, k_ref, v_ref, o_ref, run_ref, qi, *, tk, rq):
    tq = q_ref.shape[0]
    heads = q_ref.shape[1] // HEAD_DIM
    blocks_per_step = tq // tk
    chunks = [(hd, r) for hd in range(heads) for r in range(tq // rq)]
    jj = lax.broadcasted_iota(jnp.int32, (tk, tk), 0)
    ss = lax.broadcasted_iota(jnp.int32, (tk, tk), 1)
    suffix = (jj > ss).astype(BF16)
    sign_bit = jnp.int32(-2 ** 31)

    def step(first, masked):
        row = lax.broadcasted_iota(jnp.int32, (rq, tk), 0)
        col = lax.broadcasted_iota(jnp.int32, (rq, tk), 1)
        tiles = []
        for d in reversed(range(blocks_per_step)):
            for u, (_, r) in enumerate(chunks):
                if masked and d * tk >= (r + 1) * rq - 1:
                    continue
                partial = masked and (d + 1) * tk - 1 >= r * rq
                tiles.append((u, d, (col + d * tk < row + r * rq) if partial else None))

        def kv_rows(d):
            return pl.ds(pl.multiple_of((first + d) * tk, tk), tk)

        def cols(u):
            hd = chunks[u][0]
            return slice(hd * HEAD_DIM, (hd + 1) * HEAD_DIM)

        def rows(u):
            r = chunks[u][1]
            return slice(r * rq, (r + 1) * rq)

        runs = [jnp.zeros((rq, LANES), F32) if masked else None for _ in chunks]
        started = [not masked for _ in chunks]
        n = len(tiles)
        zs, log_betas, tails = [None] * n, [None] * n, [None] * n

        def scores(i):
            u, d, _ = tiles[i]
            zs[i] = lax.dot_general(q_ref[rows(u), cols(u)], k_ref[kv_rows(d), cols(u)],
                                    (((1,), (1,)), ((), ())), preferred_element_type=F32)

        def log_terms(i):
            u, d, mask = tiles[i]
            z = zs[i]
            neg_abs = pltpu.bitcast(pltpu.bitcast(z, jnp.int32) | sign_bit, F32)
            softplus = jnp.log(1.0 + jnp.exp2(neg_abs)) * LOG2_E
            log_beta = jnp.minimum(z, 0.0) - softplus
            log_om = log_beta - z
            if mask is not None:
                log_om = jnp.where(mask, log_om, 0.0)
            if runs[u] is None:
                runs[u] = run_ref[u]
            log_betas[i] = log_beta + jnp.tile(runs[u], (1, tk // LANES))
            runs[u] = runs[u] + jnp.sum(log_om, axis=-1, keepdims=True)
            tails[i] = jnp.dot(log_om.astype(BF16), suffix, preferred_element_type=F32)

        def accumulate(i):
            u, d, mask = tiles[i]
            a = jnp.exp2(log_betas[i] + tails[i])
            if mask is not None:
                a = jnp.where(mask, a, 0.0)
            av = jnp.dot(a.astype(BF16), v_ref[kv_rows(d), cols(u)], preferred_element_type=F32)
            o_ref[rows(u), cols(u)] = o_ref[rows(u), cols(u)] + av if started[u] else av
            started[u] = True

        group, lag = 2, 1
        groups = [range(s, min(s + group, n)) for s in range(0, n, group)]
        for t in range(len(groups) + 1 + lag):
            for stage, back in ((scores, 0), (log_terms, 1), (accumulate, 1 + lag)):
                if 0 <= t - back < len(groups):
                    for i in groups[t - back]:
                        stage(i)
        for u in range(len(chunks)):
            if runs[u] is not None:
                run_ref[u] = runs[u]
            if not started[u]:
                o_ref[rows(u), cols(u)] = jnp.zeros((rq, HEAD_DIM), F32)
        return jnp.max(functools.reduce(
            jnp.maximum, [run_ref[u] if run is None else run for u, run in enumerate(runs)]))

    top = step(qi * blocks_per_step, True)

    def body(state):
        n, _ = state
        return n + 1, step((qi - 1 - n) * blocks_per_step, False)

    lax.while_loop(lambda state: (state[0] < qi) & (state[1] > SB_DEAD_LOG2), body,
                   (jnp.int32(0), top))


def _sb_attention(q, k, v, *, batch, seq, tq=256, tk=256, rq=128, heads_per_step=8,
                  q_blocks_per_step=4):
    M = q.shape[0]
    t = min(tq, seq)
    tk = min(tk, t)
    rq = min(rq, t)
    q_blocks_per_step = min(q_blocks_per_step, seq // t)
    rows = t * q_blocks_per_step
    nq = seq // rows
    width = heads_per_step * HEAD_DIM
    k3 = k.reshape(batch, seq, k.shape[1])
    v3 = v.reshape(batch, seq, v.shape[1])
    return pl.pallas_call(
        functools.partial(_sb_attn_kernel, tq=t, tk=tk, rq=rq),
        scratch_shapes=[pltpu.VMEM((heads_per_step * (t // rq), rq, LANES), F32)],
        grid=(batch, SB_HEADS // heads_per_step, nq),
        in_specs=[
            pl.BlockSpec((rows, width), lambda b, h, i: (b * nq + i, h)),
            pl.BlockSpec((None, seq, width), lambda b, h, i: (b, 0, h)),
            pl.BlockSpec((None, seq, width), lambda b, h, i: (b, 0, h)),
        ],
        out_specs=pl.BlockSpec((rows, width), lambda b, h, i: (b * nq + i, h)),
        out_shape=jax.ShapeDtypeStruct((M, SB_HEADS * HEAD_DIM), F32),
        compiler_params=_params("arbitrary", "arbitrary", "arbitrary"),
        name="sb_attn",
    )(q, k3, v3)


def _outproj_kernel(x_ref, om_ref, os_ref, gt_ref, mg_ref, sg_ref, wom_ref, wos_ref,
                    lng_ref, lnb_ref, o_ref, *, alpha):
    rows_total = x_ref.shape[0]
    chunk = min(OUTPROJ_ROW_CHUNK, rows_total)
    gate = (1.0 + gt_ref[...]) / alpha
    for r0 in range(0, rows_total, chunk):
        rows = slice(r0, r0 + chunk)
        om = _rms_norm(om_ref[rows, :], mg_ref[...]).astype(BF16)
        os_ = _rms_norm(os_ref[rows, :], sg_ref[...]).astype(BF16)
        m = (jnp.dot(om, wom_ref[...], preferred_element_type=F32)
             + jnp.dot(os_, wos_ref[...], preferred_element_type=F32))
        y = x_ref[rows, :] + gate * m
        o_ref[rows, :] = _layer_norm(y, lng_ref[...], lnb_ref[...], LN_EPS / (alpha * alpha))


def _outproj_block(x, o_mla, o_sb, ada, l, k_gate, w, ln_g, ln_b, k_ln, *, alpha,
                   rows_per_batch, tm=512):
    M, D = x.shape
    tm = min(tm, rows_per_batch)
    tiles_per_batch = rows_per_batch // tm

    def rows(width):
        return pl.BlockSpec((tm, width), lambda i: (i, 0))

    consts = [w["mla_out_g"], w["sb_out_g"], w["o_mla"], w["o_sb"]]
    ln_spec = pl.BlockSpec((None, None, 1, D), lambda i: (l, k_ln, 0, 0))
    return pl.pallas_call(
        functools.partial(_outproj_kernel, alpha=alpha),
        grid=(M // tm,),
        in_specs=[rows(D), rows(o_mla.shape[1]), rows(o_sb.shape[1]),
                  pl.BlockSpec((None, None, None, 1, D),
                               lambda i: (l, i // tiles_per_batch, k_gate, 0, 0))]
                 + [_layer_resident(a, l) for a in consts] + [ln_spec, ln_spec],
        out_specs=rows(D),
        out_shape=jax.ShapeDtypeStruct((M, D), F32),
        compiler_params=_params("arbitrary"),
        name="outproj",
    )(x, o_mla, o_sb, ada, *consts, ln_g, ln_b)


def _mixer_weights(w_in, q_norm_g, kv_norm_g, w_uq, w_ukv, mla_out_g, sb_out_g, w_o):
    depth = w_in.shape[0]
    o1 = Q_LORA_RANK
    o2 = o1 + KV_LORA_RANK
    o3 = o2 + QK_ROPE_DIM
    lane_pad = ((0, 0),) * 2 + ((0, LANES - QK_ROPE_DIM),)
    uq = w_uq.astype(BF16).reshape(depth, Q_LORA_RANK, MLA_HEADS, MLA_QK_DIM)
    uq_rope = jnp.pad(uq[..., QK_NOPE_DIM:], ((0, 0),) + lane_pad)
    ukv = w_ukv.astype(BF16).reshape(depth, KV_LORA_RANK, MLA_HEADS, QK_NOPE_DIM + HEAD_DIM)
    mla_w = MLA_HEADS * HEAD_DIM
    w_in = w_in.astype(BF16)
    return {
        "cq": w_in[:, :, :o1],
        "ckv": w_in[:, :, o1:o2],
        "kr": jnp.pad(w_in[:, :, o2:o3], lane_pad),
        "sb": w_in[:, :, o3:],
        "qn_g": q_norm_g.reshape(depth, 1, -1),
        "kvn_g": kv_norm_g.reshape(depth, 1, -1),
        "uq_nope": uq[..., :QK_NOPE_DIM].reshape(depth, Q_LORA_RANK, MLA_HEADS * QK_NOPE_DIM),
        "uq_rope": uq_rope.reshape(depth, Q_LORA_RANK, MLA_HEADS * LANES),
        "uk": ukv[..., :QK_NOPE_DIM].reshape(depth, KV_LORA_RANK, MLA_HEADS * QK_NOPE_DIM),
        "uv": ukv[..., QK_NOPE_DIM:].reshape(depth, KV_LORA_RANK, MLA_HEADS * HEAD_DIM),
        "mla_out_g": mla_out_g.reshape(depth, 1, -1),
        "sb_out_g": sb_out_g.reshape(depth, 1, -1),
        "o_mla": w_o[:, :mla_w].astype(BF16),
        "o_sb": w_o[:, mla_w:].astype(BF16),
    }


def kernel(x, c, positions, ada_w, ada_b, ln_g, ln_b, ffn1_wi, ffn1_wo, w_in, q_norm_g, kv_norm_g, w_uq, w_ukv, mla_out_g, sb_out_g, w_o, ffn2_wi, ffn2_wo):
    B, S, D = x.shape
    depth = ada_w.shape[0]
    alpha = (2.0 * depth) ** 0.25
    M = B * S

    ada = _ada_all_layers(c, ada_w, ada_b).reshape(depth, B, N_ADA, 1, D)
    tables = _rope_tables(positions)
    ln_g = ln_g.reshape(depth, 3, 1, D)
    ln_b = ln_b.reshape(depth, 3, 1, D)
    ffn1 = (ffn1_wi.astype(BF16), ffn1_wo.astype(BF16))
    ffn2 = (ffn2_wi.astype(BF16), ffn2_wo.astype(BF16))
    w = _mixer_weights(w_in, q_norm_g, kv_norm_g, w_uq, w_ukv, mla_out_g, sb_out_g, w_o)
    xf = x.reshape(M, D)
    for l in range(depth):
        xf = _ffn_block(xf, ada, l, 0, *ffn1, ln_g, ln_b, 0, alpha=alpha, rows_per_batch=S)
        q, k, v, sq, sk, sv = _inproj_block(xf, ada, l, 3, tables, w, rows_per_batch=S)
        o_mla = _mla_attention(q, k, v, batch=B, seq=S)
        o_sb = _sb_attention(sq, sk, sv, batch=B, seq=S)
        xf = _outproj_block(xf, o_mla, o_sb, ada, l, 5, w, ln_g, ln_b, 1,
                            alpha=alpha, rows_per_batch=S)
        xf = _ffn_block(xf, ada, l, 6, *ffn2, ln_g, ln_b, 2, alpha=alpha, rows_per_batch=S)
    return xf.reshape(B, S, D)
```

```python
import functools

import jax
import jax.numpy as jnp
from jax import lax
from jax.experimental import pallas as pl
from jax.experimental.pallas import tpu as pltpu

HEAD_DIM = 128
MLA_HEADS = 8
SB_HEADS = 8
Q_LORA_RANK = 768
KV_LORA_RANK = 512
QK_NOPE_DIM = 128
QK_ROPE_DIM = 64
MLA_QK_DIM = QK_NOPE_DIM + QK_ROPE_DIM
ROPE_THETA = 10000.0
N_ADA = 9
LN_EPS = 1e-5
RMS_EPS = 1e-6
FFN_RESIDUAL_WEIGHT = 0.5
LOG2_E = 1.4426950408889634
SB_DEAD_LOG2 = -152.0

LANES = 128
MLA_QK_PAD = 2 * LANES
VMEM_LIMIT_BYTES = 60 * 1024 * 1024
OUTPROJ_ROW_CHUNK = 256
FFN_EPILOGUE_ROW_CHUNK = 256

F32 = jnp.float32
BF16 = jnp.bfloat16


def _params(*semantics):
    return pltpu.CompilerParams(dimension_semantics=semantics,
                                vmem_limit_bytes=VMEM_LIMIT_BYTES)


def _layer_resident(stacked, l):
    index = (l,) + (0,) * (stacked.ndim - 1)
    return pl.BlockSpec((None,) + stacked.shape[1:], lambda *_: index,
                        pipeline_mode=pl.Buffered(1))


def _layer_norm(y, g, b, eps=LN_EPS):
    mu = jnp.mean(y, axis=-1, keepdims=True)
    d = y - mu
    var = jnp.mean(d * d, axis=-1, keepdims=True)
    return d * lax.rsqrt(var + eps) * g + b


def _rms_norm(y, g):
    return y * lax.rsqrt(jnp.mean(y * y, axis=-1, keepdims=True) + RMS_EPS) * g


def _ada_kernel(c_ref, w_ref, b_ref, o_ref):
    c = c_ref[...]
    c_act = (c * jax.nn.sigmoid(c)).astype(BF16)
    acc = jnp.dot(c_act, w_ref[...].astype(BF16), preferred_element_type=F32)
    o_ref[...] = acc + b_ref[...]


def _ada_all_layers(c, ada_w, ada_b, tn=1024):
    L, D, N = ada_w.shape
    B = c.shape[0]
    tn = min(tn, D)
    assert D % tn == 0
    return pl.pallas_call(
        _ada_kernel,
        grid=(L, N // tn),
        in_specs=[
            pl.BlockSpec((B, D), lambda l, j: (0, 0)),
            pl.BlockSpec((None, D, tn), lambda l, j: (l, 0, j)),
            pl.BlockSpec((None, 1, tn), lambda l, j: (l, 0, j)),
        ],
        out_specs=pl.BlockSpec((None, B, tn), lambda l, j: (l, 0, j)),
        out_shape=jax.ShapeDtypeStruct((L, B, N), F32),
        compiler_params=_params("arbitrary", "arbitrary"),
        name="ada",
    )(c, ada_w, ada_b.reshape(L, 1, N))


def _rope_table_kernel(pos_ref, invf_ref, cos_ref, sinlo_ref, sinhi_ref):
    half = QK_ROPE_DIM // 2
    ang = pos_ref[...].astype(F32) * invf_ref[...]
    lane = lax.broadcasted_iota(jnp.int32, ang.shape, 1)
    cos = jnp.cos(ang)
    sin = jnp.sin(ang)
    cos_ref[...] = jnp.where(lane < QK_ROPE_DIM, cos, 0.0)
    sinlo_ref[...] = jnp.where(lane < half, -sin, 0.0)
    sinhi_ref[...] = jnp.where((lane >= half) & (lane < QK_ROPE_DIM), sin, 0.0)


def _rope_tables(positions, tm=1024):
    M = positions.size
    half = QK_ROPE_DIM // 2
    inv_freq = ROPE_THETA ** (-jnp.arange(half, dtype=F32) / half)
    invf = jnp.tile(inv_freq, LANES // half).reshape(1, LANES)
    tm = min(tm, M)
    spec = pl.BlockSpec((tm, LANES), lambda i: (i, 0))
    shape = jax.ShapeDtypeStruct((M, LANES), F32)
    return pl.pallas_call(
        _rope_table_kernel,
        grid=(M // tm,),
        in_specs=[pl.BlockSpec((tm, 1), lambda i: (i, 0)),
                  pl.BlockSpec((1, LANES), lambda i: (0, 0))],
        out_specs=[spec, spec, spec],
        out_shape=[shape, shape, shape],
        compiler_params=_params("arbitrary"),
        name="rope_tables",
    )(positions.reshape(M, 1), invf)


def _ffn_kernel(x_ref, sh_ref, sc_ref, gt_ref, wg_ref, wu_ref, wo_ref, lng_ref, lnb_ref,
                o_ref, *, alpha, hidden_tiles):
    j = pl.program_id(1)
    last = hidden_tiles - 1

    def gated():
        h = (x_ref[...] * (1.0 + sc_ref[...]) + sh_ref[...]).astype(BF16)
        g = jnp.dot(h, wg_ref[...], preferred_element_type=F32)
        u = jnp.dot(h, wu_ref[...], preferred_element_type=F32)
        return (g * jax.nn.sigmoid(g) * u).astype(BF16)

    if hidden_tiles > 1:
        @pl.when(j == 0)
        def _():
            o_ref[...] = jnp.dot(gated(), wo_ref[...], preferred_element_type=F32)

        @pl.when((j > 0) & (j < last))
        def _():
            o_ref[...] += jnp.dot(gated(), wo_ref[...], preferred_element_type=F32)

    @pl.when(j == last)
    def _():
        a = gated()
        c = (FFN_RESIDUAL_WEIGHT / alpha) * (1.0 + gt_ref[...])
        chunk = min(FFN_EPILOGUE_ROW_CHUNK, a.shape[0])
        for r0 in range(0, a.shape[0], chunk):
            rows = slice(r0, r0 + chunk)
            f = jnp.dot(a[rows], wo_ref[...], preferred_element_type=F32)
            if hidden_tiles > 1:
                f = o_ref[rows, :] + f
            y = x_ref[rows, :] + c * f
            o_ref[rows, :] = _layer_norm(y, lng_ref[...], lnb_ref[...], LN_EPS / (alpha * alpha))


def _ffn_block(x, ada, l, k_shift, wi, wo, ln_g, ln_b, k_ln, *, alpha, rows_per_batch,
               tm=1024, tf=512):
    M, D = x.shape
    F = wo.shape[1]
    tm = min(tm, rows_per_batch)
    tf = min(tf, F)
    nf = F // tf
    tiles_per_batch = rows_per_batch // tm

    def ada_spec(k):
        return pl.BlockSpec((None, None, None, 1, D),
                            lambda i, j: (l, i // tiles_per_batch, k, 0, 0))

    ln_spec = pl.BlockSpec((None, None, 1, D), lambda i, j: (l, k_ln, 0, 0))
    return pl.pallas_call(
        functools.partial(_ffn_kernel, alpha=alpha, hidden_tiles=nf),
        grid=(M // tm, nf),
        in_specs=[
            pl.BlockSpec((tm, D), lambda i, j: (i, 0)),
            ada_spec(k_shift), ada_spec(k_shift + 1), ada_spec(k_shift + 2),
            pl.BlockSpec((None, D, tf), lambda i, j: (l, 0, j)),
            pl.BlockSpec((None, D, tf), lambda i, j: (l, 0, j + nf)),
            pl.BlockSpec((None, tf, D), lambda i, j: (l, j, 0)),
            ln_spec, ln_spec,
        ],
        out_specs=pl.BlockSpec((tm, D), lambda i, j: (i, 0)),
        out_shape=jax.ShapeDtypeStruct((M, D), F32),
        compiler_params=_params("arbitrary", "arbitrary"),
        name="ffn",
    )(x, ada, ada, ada, wi, wi, wo, ln_g, ln_b)


def _inproj_kernel(x_ref, sh_ref, sc_ref, cos_ref, sinlo_ref, sinhi_ref,
                   wcq_ref, wckv_ref, wkr_ref, wsb_ref, qng_ref, kvng_ref,
                   wuqn_ref, wuqr_ref, wuk_ref, wuv_ref,
                   q_ref, k_ref, v_ref, sq_ref, sk_ref, sv_ref):
    h = (x_ref[...] * (1.0 + sc_ref[...]) + sh_ref[...]).astype(BF16)
    cos, sin_lo, sin_hi = cos_ref[...], sinlo_ref[...], sinhi_ref[...]
    half = QK_ROPE_DIM // 2

    def rope(xr):
        return (xr * cos + pltpu.roll(xr, LANES - half, axis=1) * sin_lo
                + pltpu.roll(xr, half, axis=1) * sin_hi)

    c_q = jnp.dot(h, wcq_ref[...], preferred_element_type=F32)
    c_kv = jnp.dot(h, wckv_ref[...], preferred_element_type=F32)
    k_rope = jnp.dot(h, wkr_ref[...], preferred_element_type=F32)

    sb_w = SB_HEADS * HEAD_DIM
    sb = jnp.dot(h, wsb_ref[...], preferred_element_type=F32)
    sq_ref[...] = (sb[:, :sb_w] * (LOG2_E * HEAD_DIM ** -0.5)).astype(BF16)
    sk_ref[...] = sb[:, sb_w:2 * sb_w].astype(BF16)
    sv_ref[...] = sb[:, 2 * sb_w:].astype(BF16)

    c_qn = _rms_norm(c_q, qng_ref[...]).astype(BF16)
    c_kvn = _rms_norm(c_kv, kvng_ref[...]).astype(BF16)
    q_nope = jnp.dot(c_qn, wuqn_ref[...], preferred_element_type=F32)
    q_rope = jnp.dot(c_qn, wuqr_ref[...], preferred_element_type=F32)
    k_nope = jnp.dot(c_kvn, wuk_ref[...], preferred_element_type=F32)
    v_ref[...] = jnp.dot(c_kvn, wuv_ref[...], preferred_element_type=F32).astype(BF16)
    q_scale = LOG2_E * MLA_QK_DIM ** -0.5
    for hd in range(MLA_HEADS):
        lo = hd * MLA_QK_PAD
        q_ref[:, lo:lo + LANES] = (q_nope[:, hd * LANES:(hd + 1) * LANES] * q_scale).astype(BF16)
        q_ref[:, lo + LANES:lo + 2 * LANES] = (
            rope(q_rope[:, hd * LANES:(hd + 1) * LANES]) * q_scale).astype(BF16)

    k_pe = rope(k_rope).astype(BF16)
    for hd in range(MLA_HEADS):
        lo = hd * MLA_QK_PAD
        k_ref[:, lo:lo + LANES] = k_nope[:, hd * LANES:(hd + 1) * LANES].astype(BF16)
        k_ref[:, lo + LANES:lo + 2 * LANES] = k_pe


def _inproj_block(x, ada, l, k_shift, tables, w, *, rows_per_batch, tm=256):
    M, D = x.shape
    tm = min(tm, rows_per_batch)
    tiles_per_batch = rows_per_batch // tm

    def ada_spec(k):
        return pl.BlockSpec((None, None, None, 1, D),
                            lambda i: (l, i // tiles_per_batch, k, 0, 0))

    def rows(width):
        return pl.BlockSpec((tm, width), lambda i: (i, 0))

    weights = [w["cq"], w["ckv"], w["kr"], w["sb"], w["qn_g"], w["kvn_g"],
               w["uq_nope"], w["uq_rope"], w["uk"], w["uv"]]
    widths = [MLA_HEADS * MLA_QK_PAD, MLA_HEADS * MLA_QK_PAD, MLA_HEADS * HEAD_DIM,
              SB_HEADS * HEAD_DIM, SB_HEADS * HEAD_DIM, SB_HEADS * HEAD_DIM]
    return pl.pallas_call(
        _inproj_kernel,
        grid=(M // tm,),
        in_specs=[rows(D), ada_spec(k_shift), ada_spec(k_shift + 1),
                  rows(LANES), rows(LANES), rows(LANES)]
                 + [_layer_resident(a, l) for a in weights],
        out_specs=[rows(n) for n in widths],
        out_shape=[jax.ShapeDtypeStruct((M, n), BF16) for n in widths],
        compiler_params=_params("arbitrary"),
        name="inproj",
    )(x, ada, ada, *tables, *weights)


def _mla_attn_kernel(q_ref, k_ref, v_ref, o_ref, m_ref, l_ref, acc_ref):
    tq = tk = m_ref.shape[1]
    for g in range(q_ref.shape[0] // tq):
        _mla_sweep(q_ref.at[g * tq:(g + 1) * tq], k_ref, v_ref, o_ref.at[g * tq:(g + 1) * tq],
                   m_ref, l_ref, acc_ref, pl.program_id(2) * (q_ref.shape[0] // tq) + g)


def _mla_sweep(q_ref, k_ref, v_ref, o_ref, m_ref, l_ref, acc_ref, qi):
    tq = tk = q_ref.shape[0]
    heads = q_ref.shape[1] // MLA_QK_PAD
    reps = tk // LANES

    def step(c, first):
        kv_rows = pl.ds(pl.multiple_of(c * tk, tk), tk)
        scores = [lax.dot_general(q_ref[:, hd * MLA_QK_PAD:(hd + 1) * MLA_QK_PAD],
                                  k_ref[kv_rows, hd * MLA_QK_PAD:(hd + 1) * MLA_QK_PAD],
                                  (((1,), (1,)), ((), ())), preferred_element_type=F32)
                  for hd in range(heads)]
        probs, corrs = [], []
        for hd, s in enumerate(scores):
            if first:
                row = lax.broadcasted_iota(jnp.int32, (tq, tk), 0)
                col = lax.broadcasted_iota(jnp.int32, (tq, tk), 1)
                s = jnp.where(col <= row, s, -jnp.inf)
                m_new = jnp.broadcast_to(jnp.max(s, axis=-1, keepdims=True), (tq, LANES))
                p = jnp.exp2(s - jnp.tile(m_new, (1, reps)))
                l_ref[hd] = jnp.broadcast_to(jnp.sum(p, axis=-1, keepdims=True), (tq, LANES))
                corrs.append(None)
            else:
                m_prev = m_ref[hd]
                m_new = jnp.maximum(m_prev, jnp.max(s, axis=-1, keepdims=True))
                corr = jnp.exp2(m_prev - m_new)
                p = jnp.exp2(s - jnp.tile(m_new, (1, reps)))
                l_ref[hd] = l_ref[hd] * corr + jnp.sum(p, axis=-1, keepdims=True)
                corrs.append(corr)
            m_ref[hd] = m_new
            probs.append(p.astype(BF16))
        for hd, (p, corr) in enumerate(zip(probs, corrs)):
            pv = jnp.dot(p, v_ref[kv_rows, hd * HEAD_DIM:(hd + 1) * HEAD_DIM],
                         preferred_element_type=F32)
            acc_ref[hd] = pv if first else acc_ref[hd] * corr + pv

    step(qi, True)

    def body(c, carry):
        step(c, False)
        return carry

    lax.fori_loop(0, qi, body, 0)
    for hd in range(heads):
        o_ref[:, hd * HEAD_DIM:(hd + 1) * HEAD_DIM] = acc_ref[hd] / l_ref[hd]


def _mla_attention(q, k, v, *, batch, seq, t=512, heads_per_step=4, q_blocks_per_step=4):
    M = q.shape[0]
    t = min(t, seq)
    q_blocks_per_step = min(q_blocks_per_step, seq // t)
    tq = t * q_blocks_per_step
    nq = seq // tq
    k3 = k.reshape(batch, seq, k.shape[1])
    v3 = v.reshape(batch, seq, v.shape[1])
    qk_width = heads_per_step * MLA_QK_PAD
    v_width = heads_per_step * HEAD_DIM
    return pl.pallas_call(
        _mla_attn_kernel,
        scratch_shapes=[pltpu.VMEM((heads_per_step, t, LANES), F32),
                        pltpu.VMEM((heads_per_step, t, LANES), F32),
                        pltpu.VMEM((heads_per_step, t, HEAD_DIM), F32)],
        grid=(batch, MLA_HEADS // heads_per_step, nq),
        in_specs=[
            pl.BlockSpec((tq, qk_width), lambda b, h, i: (b * nq + i, h)),
            pl.BlockSpec((None, seq, qk_width), lambda b, h, i: (b, 0, h)),
            pl.BlockSpec((None, seq, v_width), lambda b, h, i: (b, 0, h)),
        ],
        out_specs=pl.BlockSpec((tq, v_width), lambda b, h, i: (b * nq + i, h)),
        out_shape=jax.ShapeDtypeStruct((M, MLA_HEADS * HEAD_DIM), F32),
        compiler_params=_params("arbitrary", "arbitrary", "arbitrary"),
        name="mla_attn",
    )(q, k3, v3)


def _sb_attn_kernel(q_ref, k_ref, v_ref, o_ref, run_ref, *, tq, tk, rq):
    for g in range(q_ref.shape[0] // tq):
        _sb_sweep(q_ref.at[g * tq:(g + 1) * tq], k_ref, v_ref, o_ref.at[g * tq:(g + 1) * tq],
                  run_ref, pl.program_id(2) * (q_ref.shape[0] // tq) + g, tk=tk, rq=rq)


def _sb_sweep(q_ref, k_ref, v_ref, o_ref, run_ref, qi, *, tk, rq):
    tq = q_ref.shape[0]
    heads = q_ref.shape[1] // HEAD_DIM
    blocks_per_step = tq // tk
    chunks = [(hd, r) for hd in range(heads) for r in range(tq // rq)]
    jj = lax.broadcasted_iota(jnp.int32, (tk, tk), 0)
    ss = lax.broadcasted_iota(jnp.int32, (tk, tk), 1)
    suffix = (jj > ss).astype(BF16)
    sign_bit = jnp.int32(-2 ** 31)

    def step(first, masked):
        row = lax.broadcasted_iota(jnp.int32, (rq, tk), 0)
        col = lax.broadcasted_iota(jnp.int32, (rq, tk), 1)
        tiles = []
        for d in reversed(range(blocks_per_step)):
            for u, (_, r) in enumerate(chunks):
                if masked and d * tk >= (r + 1) * rq - 1:
                    continue
                partial = masked and (d + 1) * tk - 1 >= r * rq
                tiles.append((u, d, (col + d * tk < row + r * rq) if partial else None))

        def kv_rows(d):
            return pl.ds(pl.multiple_of((first + d) * tk, tk), tk)

        def cols(u):
            hd = chunks[u][0]
            return slice(hd * HEAD_DIM, (hd + 1) * HEAD_DIM)

        def rows(u):
            r = chunks[u][1]
            return slice(r * rq, (r + 1) * rq)

        runs = [jnp.zeros((rq, LANES), F32) if masked else None for _ in chunks]
        started = [not masked for _ in chunks]
        n = len(tiles)
        zs, log_betas, tails = [None] * n, [None] * n, [None] * n

        def scores(i):
            u, d, _ = tiles[i]
            zs[i] = lax.dot_general(q_ref[rows(u), cols(u)], k_ref[kv_rows(d), cols(u)],
                                    (((1,), (1,)), ((), ())), preferred_element_type=F32)

        def log_terms(i):
            u, d, mask = tiles[i]
            z = zs[i]
            neg_abs = pltpu.bitcast(pltpu.bitcast(z, jnp.int32) | sign_bit, F32)
            softplus = jnp.log(1.0 + jnp.exp2(neg_abs)) * LOG2_E
            log_beta = jnp.minimum(z, 0.0) - softplus
            log_om = log_beta - z
            if mask is not None:
                log_om = jnp.where(mask, log_om, 0.0)
            if runs[u] is None:
                runs[u] = run_ref[u]
            log_betas[i] = log_beta + jnp.tile(runs[u], (1, tk // LANES))
            runs[u] = runs[u] + jnp.sum(log_om, axis=-1, keepdims=True)
            tails[i] = jnp.dot(log_om.astype(BF16), suffix, preferred_element_type=F32)

        def accumulate(i):
            u, d, mask = tiles[i]
            a = jnp.exp2(log_betas[i] + tails[i])
            if mask is not None:
                a = jnp.where(mask, a, 0.0)
            av = jnp.dot(a.astype(BF16), v_ref[kv_rows(d), cols(u)], preferred_element_type=F32)
            o_ref[rows(u), cols(u)] = o_ref[rows(u), cols(u)] + av if started[u] else av
            started[u] = True

        group, lag = 2, 1
        groups = [range(s, min(s + group, n)) for s in range(0, n, group)]
        for t in range(len(groups) + 1 + lag):
            for stage, back in ((scores, 0), (log_terms, 1), (accumulate, 1 + lag)):
                if 0 <= t - back < len(groups):
                    for i in groups[t - back]:
                        stage(i)
        for u in range(len(chunks)):
            if runs[u] is not None:
                run_ref[u] = runs[u]
            if not started[u]:
                o_ref[rows(u), cols(u)] = jnp.zeros((rq, HEAD_DIM), F32)
        return jnp.max(functools.reduce(
            jnp.maximum, [run_ref[u] if run is None else run for u, run in enumerate(runs)]))

    top = step(qi * blocks_per_step, True)

    def body(state):
        n, _ = state
        return n + 1, step((qi - 1 - n) * blocks_per_step, False)

    lax.while_loop(lambda state: (state[0] < qi) & (state[1] > SB_DEAD_LOG2), body,
                   (jnp.int32(0), top))


def _sb_attention(q, k, v, *, batch, seq, tq=256, tk=256, rq=128, heads_per_step=8,
                  q_blocks_per_step=4):
    M = q.shape[0]
    t = min(tq, seq)
    tk = min(tk, t)
    rq = min(rq, t)
    q_blocks_per_step = min(q_blocks_per_step, seq // t)
    rows = t * q_blocks_per_step
    nq = seq // rows
    width = heads_per_step * HEAD_DIM
    k3 = k.reshape(batch, seq, k.shape[1])
    v3 = v.reshape(batch, seq, v.shape[1])
    return pl.pallas_call(
        functools.partial(_sb_attn_kernel, tq=t, tk=tk, rq=rq),
        scratch_shapes=[pltpu.VMEM((heads_per_step * (t // rq), rq, LANES), F32)],
        grid=(batch, SB_HEADS // heads_per_step, nq),
        in_specs=[
            pl.BlockSpec((rows, width), lambda b, h, i: (b * nq + i, h)),
            pl.BlockSpec((None, seq, width), lambda b, h, i: (b, 0, h)),
            pl.BlockSpec((None, seq, width), lambda b, h, i: (b, 0, h)),
        ],
        out_specs=pl.BlockSpec((rows, width), lambda b, h, i: (b * nq + i, h)),
        out_shape=jax.ShapeDtypeStruct((M, SB_HEADS * HEAD_DIM), F32),
        compiler_params=_params("arbitrary", "arbitrary", "arbitrary"),
        name="sb_attn",
    )(q, k3, v3)


def _outproj_kernel(x_ref, om_ref, os_ref, gt_ref, mg_ref, sg_ref, wom_ref, wos_ref,
                    lng_ref, lnb_ref, o_ref, *, alpha):
    rows_total = x_ref.shape[0]
    chunk = min(OUTPROJ_ROW_CHUNK, rows_total)
    gate = (1.0 + gt_ref[...]) / alpha
    for r0 in range(0, rows_total, chunk):
        rows = slice(r0, r0 + chunk)
        om = _rms_norm(om_ref[rows, :], mg_ref[...]).astype(BF16)
        os_ = _rms_norm(os_ref[rows, :], sg_ref[...]).astype(BF16)
        m = (jnp.dot(om, wom_ref[...], preferred_element_type=F32)
             + jnp.dot(os_, wos_ref[...], preferred_element_type=F32))
        y = x_ref[rows, :] + gate * m
        o_ref[rows, :] = _layer_norm(y, lng_ref[...], lnb_ref[...], LN_EPS / (alpha * alpha))


def _outproj_block(x, o_mla, o_sb, ada, l, k_gate, w, ln_g, ln_b, k_ln, *, alpha,
                   rows_per_batch, tm=512):
    M, D = x.shape
    tm = min(tm, rows_per_batch)
    tiles_per_batch = rows_per_batch // tm

    def rows(width):
        return pl.BlockSpec((tm, width), lambda i: (i, 0))

    consts = [w["mla_out_g"], w["sb_out_g"], w["o_mla"], w["o_sb"]]
    ln_spec = pl.BlockSpec((None, None, 1, D), lambda i: (l, k_ln, 0, 0))
    return pl.pallas_call(
        functools.partial(_outproj_kernel, alpha=alpha),
        grid=(M // tm,),
        in_specs=[rows(D), rows(o_mla.shape[1]), rows(o_sb.shape[1]),
                  pl.BlockSpec((None, None, None, 1, D),
                               lambda i: (l, i // tiles_per_batch, k_gate, 0, 0))]
                 + [_layer_resident(a, l) for a in consts] + [ln_spec, ln_spec],
        out_specs=rows(D),
        out_shape=jax.ShapeDtypeStruct((M, D), F32),
        compiler_params=_params("arbitrary"),
        name="outproj",
    )(x, o_mla, o_sb, ada, *consts, ln_g, ln_b)


def _mixer_weights(w_in, q_norm_g, kv_norm_g, w_uq, w_ukv, mla_out_g, sb_out_g, w_o):
    depth = w_in.shape[0]
    o1 = Q_LORA_RANK
    o2 = o1 + KV_LORA_RANK
    o3 = o2 + QK_ROPE_DIM
    lane_pad = ((0, 0),) * 2 + ((0, LANES - QK_ROPE_DIM),)
    uq = w_uq.astype(BF16).reshape(depth, Q_LORA_RANK, MLA_HEADS, MLA_QK_DIM)
    uq_rope = jnp.pad(uq[..., QK_NOPE_DIM:], ((0, 0),) + lane_pad)
    ukv = w_ukv.astype(BF16).reshape(depth, KV_LORA_RANK, MLA_HEADS, QK_NOPE_DIM + HEAD_DIM)
    mla_w = MLA_HEADS * HEAD_DIM
    w_in = w_in.astype(BF16)
    return {
        "cq": w_in[:, :, :o1],
        "ckv": w_in[:, :, o1:o2],
        "kr": jnp.pad(w_in[:, :, o2:o3], lane_pad),
        "sb": w_in[:, :, o3:],
        "qn_g": q_norm_g.reshape(depth, 1, -1),
        "kvn_g": kv_norm_g.reshape(depth, 1, -1),
        "uq_nope": uq[..., :QK_NOPE_DIM].reshape(depth, Q_LORA_RANK, MLA_HEADS * QK_NOPE_DIM),
        "uq_rope": uq_rope.reshape(depth, Q_LORA_RANK, MLA_HEADS * LANES),
        "uk": ukv[..., :QK_NOPE_DIM].reshape(depth, KV_LORA_RANK, MLA_HEADS * QK_NOPE_DIM),
        "uv": ukv[..., QK_NOPE_DIM:].reshape(depth, KV_LORA_RANK, MLA_HEADS * HEAD_DIM),
        "mla_out_g": mla_out_g.reshape(depth, 1, -1),
        "sb_out_g": sb_out_g.reshape(depth, 1, -1),
        "o_mla": w_o[:, :mla_w].astype(BF16),
        "o_sb": w_o[:, mla_w:].astype(BF16),
    }


def kernel(x, c, positions, ada_w, ada_b, ln_g, ln_b, ffn1_wi, ffn1_wo, w_in, q_norm_g, kv_norm_g, w_uq, w_ukv, mla_out_g, sb_out_g, w_o, ffn2_wi, ffn2_wo):
    B, S, D = x.shape
    depth = ada_w.shape[0]
    alpha = (2.0 * depth) ** 0.25
    M = B * S

    ada = _ada_all_layers(c, ada_w, ada_b).reshape(depth, B, N_ADA, 1, D)
    tables = _rope_tables(positions)
    ln_g = ln_g.reshape(depth, 3, 1, D)
    ln_b = ln_b.reshape(depth, 3, 1, D)
    ffn1 = (ffn1_wi.astype(BF16), ffn1_wo.astype(BF16))
    ffn2 = (ffn2_wi.astype(BF16), ffn2_wo.astype(BF16))
    w = _mixer_weights(w_in, q_norm_g, kv_norm_g, w_uq, w_ukv, mla_out_g, sb_out_g, w_o)
    xf = x.reshape(M, D)
    for l in range(depth):
        xf = _ffn_block(xf, ada, l, 0, *ffn1, ln_g, ln_b, 0, alpha=alpha, rows_per_batch=S)
        q, k, v, sq, sk, sv = _inproj_block(xf, ada, l, 3, tables, w, rows_per_batch=S)
        o_mla = _mla_attention(q, k, v, batch=B, seq=S)
        o_sb = _sb_attention(sq, sk, sv, batch=B, seq=S)
        xf = _outproj_block(xf, o_mla, o_sb, ada, l, 5, w, ln_g, ln_b, 1,
                            alpha=alpha, rows_per_batch=S)
        xf = _ffn_block(xf, ada, l, 6, *ffn2, ln_g, ln_b, 2, alpha=alpha, rows_per_batch=S)
    return xf.reshape(B, S, D)
```

```python
import functools

import jax
import jax.numpy as jnp
from jax import lax
from jax.experimental import pallas as pl
from jax.experimental.pallas import tpu as pltpu

HEAD_DIM = 128
MLA_HEADS = 8
SB_HEADS = 8
Q_LORA_RANK = 768
KV_LORA_RANK = 512
QK_NOPE_DIM = 128
QK_ROPE_DIM = 64
MLA_QK_DIM = QK_NOPE_DIM + QK_ROPE_DIM
ROPE_THETA = 10000.0
N_ADA = 9
LN_EPS = 1e-5
RMS_EPS = 1e-6
FFN_RESIDUAL_WEIGHT = 0.5
LOG2_E = 1.4426950408889634
SB_DEAD_LOG2 = -152.0

LANES = 128
MLA_QK_PAD = 2 * LANES
VMEM_LIMIT_BYTES = 60 * 1024 * 1024
OUTPROJ_ROW_CHUNK = 256
FFN_EPILOGUE_ROW_CHUNK = 256

F32 = jnp.float32
BF16 = jnp.bfloat16


def _params(*semantics):
    return pltpu.CompilerParams(dimension_semantics=semantics,
                                vmem_limit_bytes=VMEM_LIMIT_BYTES)


def _layer_resident(stacked, l):
    index = (l,) + (0,) * (stacked.ndim - 1)
    return pl.BlockSpec((None,) + stacked.shape[1:], lambda *_: index,
                        pipeline_mode=pl.Buffered(1))


def _layer_norm(y, g, b, eps=LN_EPS):
    mu = jnp.mean(y, axis=-1, keepdims=True)
    d = y - mu
    var = jnp.mean(d * d, axis=-1, keepdims=True)
    return d * lax.rsqrt(var + eps) * g + b


def _rms_norm(y, g):
    return y * lax.rsqrt(jnp.mean(y * y, axis=-1, keepdims=True) + RMS_EPS) * g


def _ada_kernel(c_ref, w_ref, b_ref, o_ref):
    c = c_ref[...]
    c_act = (c * jax.nn.sigmoid(c)).astype(BF16)
    acc = jnp.dot(c_act, w_ref[...].astype(BF16), preferred_element_type=F32)
    o_ref[...] = acc + b_ref[...]


def _ada_all_layers(c, ada_w, ada_b, tn=1024):
    L, D, N = ada_w.shape
    B = c.shape[0]
    tn = min(tn, D)
    assert D % tn == 0
    return pl.pallas_call(
        _ada_kernel,
        grid=(L, N // tn),
        in_specs=[
            pl.BlockSpec((B, D), lambda l, j: (0, 0)),
            pl.BlockSpec((None, D, tn), lambda l, j: (l, 0, j)),
            pl.BlockSpec((None, 1, tn), lambda l, j: (l, 0, j)),
        ],
        out_specs=pl.BlockSpec((None, B, tn), lambda l, j: (l, 0, j)),
        out_shape=jax.ShapeDtypeStruct((L, B, N), F32),
        compiler_params=_params("arbitrary", "arbitrary"),
        name="ada",
    )(c, ada_w, ada_b.reshape(L, 1, N))


def _rope_table_kernel(pos_ref, invf_ref, cos_ref, sinlo_ref, sinhi_ref):
    half = QK_ROPE_DIM // 2
    ang = pos_ref[...].astype(F32) * invf_ref[...]
    lane = lax.broadcasted_iota(jnp.int32, ang.shape, 1)
    cos = jnp.cos(ang)
    sin = jnp.sin(ang)
    cos_ref[...] = jnp.where(lane < QK_ROPE_DIM, cos, 0.0)
    sinlo_ref[...] = jnp.where(lane < half, -sin, 0.0)
    sinhi_ref[...] = jnp.where((lane >= half) & (lane < QK_ROPE_DIM), sin, 0.0)


def _rope_tables(positions, tm=1024):
    M = positions.size
    half = QK_ROPE_DIM // 2
    inv_freq = ROPE_THETA ** (-jnp.arange(half, dtype=F32) / half)
    invf = jnp.tile(inv_freq, LANES // half).reshape(1, LANES)
    tm = min(tm, M)
    spec = pl.BlockSpec((tm, LANES), lambda i: (i, 0))
    shape = jax.ShapeDtypeStruct((M, LANES), F32)
    return pl.pallas_call(
        _rope_table_kernel,
        grid=(M // tm,),
        in_specs=[pl.BlockSpec((tm, 1), lambda i: (i, 0)),
                  pl.BlockSpec((1, LANES), lambda i: (0, 0))],
        out_specs=[spec, spec, spec],
        out_shape=[shape, shape, shape],
        compiler_params=_params("arbitrary"),
        name="rope_tables",
    )(positions.reshape(M, 1), invf)


def _ffn_kernel(x_ref, sh_ref, sc_ref, gt_ref, wg_ref, wu_ref, wo_ref, lng_ref, lnb_ref,
                o_ref, *, alpha, hidden_tiles):
    j = pl.program_id(1)
    last = hidden_tiles - 1

    def gated():
        h = (x_ref[...] * (1.0 + sc_ref[...]) + sh_ref[...]).astype(BF16)
        g = jnp.dot(h, wg_ref[...], preferred_element_type=F32)
        u = jnp.dot(h, wu_ref[...], preferred_element_type=F32)
        return (g * jax.nn.sigmoid(g) * u).astype(BF16)

    if hidden_tiles > 1:
        @pl.when(j == 0)
        def _():
            o_ref[...] = jnp.dot(gated(), wo_ref[...], preferred_element_type=F32)

        @pl.when((j > 0) & (j < last))
        def _():
            o_ref[...] += jnp.dot(gated(), wo_ref[...], preferred_element_type=F32)

    @pl.when(j == last)
    def _():
        a = gated()
        c = (FFN_RESIDUAL_WEIGHT / alpha) * (1.0 + gt_ref[...])
        chunk = min(FFN_EPILOGUE_ROW_CHUNK, a.shape[0])
        for r0 in range(0, a.shape[0], chunk):
            rows = slice(r0, r0 + chunk)
            f = jnp.dot(a[rows], wo_ref[...], preferred_element_type=F32)
            if hidden_tiles > 1:
                f = o_ref[rows, :] + f
            y = x_ref[rows, :] + c * f
            o_ref[rows, :] = _layer_norm(y, lng_ref[...], lnb_ref[...], LN_EPS / (alpha * alpha))


def _ffn_block(x, ada, l, k_shift, wi, wo, ln_g, ln_b, k_ln, *, alpha, rows_per_batch,
               tm=1024, tf=512):
    M, D = x.shape
    F = wo.shape[1]
    tm = min(tm, rows_per_batch)
    tf = min(tf, F)
    nf = F // tf
    tiles_per_batch = rows_per_batch // tm

    def ada_spec(k):
        return pl.BlockSpec((None, None, None, 1, D),
                            lambda i, j: (l, i // tiles_per_batch, k, 0, 0))

    ln_spec = pl.BlockSpec((None, None, 1, D), lambda i, j: (l, k_ln, 0, 0))
    return pl.pallas_call(
        functools.partial(_ffn_kernel, alpha=alpha, hidden_tiles=nf),
        grid=(M // tm, nf),
        in_specs=[
            pl.BlockSpec((tm, D), lambda i, j: (i, 0)),
            ada_spec(k_shift), ada_spec(k_shift + 1), ada_spec(k_shift + 2),
            pl.BlockSpec((None, D, tf), lambda i, j: (l, 0, j)),
            pl.BlockSpec((None, D, tf), lambda i, j: (l, 0, j + nf)),
            pl.BlockSpec((None, tf, D), lambda i, j: (l, j, 0)),
            ln_spec, ln_spec,
        ],
        out_specs=pl.BlockSpec((tm, D), lambda i, j: (i, 0)),
        out_shape=jax.ShapeDtypeStruct((M, D), F32),
        compiler_params=_params("arbitrary", "arbitrary"),
        name="ffn",
    )(x, ada, ada, ada, wi, wi, wo, ln_g, ln_b)


def _inproj_kernel(x_ref, sh_ref, sc_ref, cos_ref, sinlo_ref, sinhi_ref,
                   wcq_ref, wckv_ref, wkr_ref, wsb_ref, qng_ref, kvng_ref,
                   wuqn_ref, wuqr_ref, wuk_ref, wuv_ref,
                   q_ref, k_ref, v_ref, sq_ref, sk_ref, sv_ref):
    h = (x_ref[...] * (1.0 + sc_ref[...]) + sh_ref[...]).astype(BF16)
    cos, sin_lo, sin_hi = cos_ref[...], sinlo_ref[...], sinhi_ref[...]
    half = QK_ROPE_DIM // 2

    def rope(xr):
        return (xr * cos + pltpu.roll(xr, LANES - half, axis=1) * sin_lo
                + pltpu.roll(xr, half, axis=1) * sin_hi)

    c_q = jnp.dot(h, wcq_ref[...], preferred_element_type=F32)
    c_kv = jnp.dot(h, wckv_ref[...], preferred_element_type=F32)
    k_rope = jnp.dot(h, wkr_ref[...], preferred_element_type=F32)

    sb_w = SB_HEADS * HEAD_DIM
    sb = jnp.dot(h, wsb_ref[...], preferred_element_type=F32)
    sq_ref[...] = (sb[:, :sb_w] * (LOG2_E * HEAD_DIM ** -0.5)).astype(BF16)
    sk_ref[...] = sb[:, sb_w:2 * sb_w].astype(BF16)
    sv_ref[...] = sb[:, 2 * sb_w:].astype(BF16)

    c_qn = _rms_norm(c_q, qng_ref[...]).astype(BF16)
    c_kvn = _rms_norm(c_kv, kvng_ref[...]).astype(BF16)
    q_nope = jnp.dot(c_qn, wuqn_ref[...], preferred_element_type=F32)
    q_rope = jnp.dot(c_qn, wuqr_ref[...], preferred_element_type=F32)
    k_nope = jnp.dot(c_kvn, wuk_ref[...], preferred_element_type=F32)
    v_ref[...] = jnp.dot(c_kvn, wuv_ref[...], preferred_element_type=F32).astype(BF16)
    q_scale = LOG2_E * MLA_QK_DIM ** -0.5
    for hd in range(MLA_HEADS):
        lo = hd * MLA_QK_PAD
        q_ref[:, lo:lo + LANES] = (q_nope[:, hd * LANES:(hd + 1) * LANES] * q_scale).astype(BF16)
        q_ref[:, lo + LANES:lo + 2 * LANES] = (
            rope(q_rope[:, hd * LANES:(hd + 1) * LANES]) * q_scale).astype(BF16)

    k_pe = rope(k_rope).astype(BF16)
    for hd in range(MLA_HEADS):
        lo = hd * MLA_QK_PAD
        k_ref[:, lo:lo + LANES] = k_nope[:, hd * LANES:(hd + 1) * LANES].astype(BF16)
        k_ref[:, lo + LANES:lo + 2 * LANES] = k_pe


def _inproj_block(x, ada, l, k_shift, tables, w, *, rows_per_batch, tm=256):
    M, D = x.shape
    tm = min(tm, rows_per_batch)
    tiles_per_batch = rows_per_batch // tm

    def ada_spec(k):
        return pl.BlockSpec((None, None, None, 1, D),
                            lambda i: (l, i // tiles_per_batch, k, 0, 0))

    def rows(width):
        return pl.BlockSpec((tm, width), lambda i: (i, 0))

    weights = [w["cq"], w["ckv"], w["kr"], w["sb"], w["qn_g"], w["kvn_g"],
               w["uq_nope"], w["uq_rope"], w["uk"], w["uv"]]
    widths = [MLA_HEADS * MLA_QK_PAD, MLA_HEADS * MLA_QK_PAD, MLA_HEADS * HEAD_DIM,
              SB_HEADS * HEAD_DIM, SB_HEADS * HEAD_DIM, SB_HEADS * HEAD_DIM]
    return pl.pallas_call(
        _inproj_kernel,
        grid=(M // tm,),
        in_specs=[rows(D), ada_spec(k_shift), ada_spec(k_shift + 1),
                  rows(LANES), rows(LANES), rows(LANES)]
                 + [_layer_resident(a, l) for a in weights],
        out_specs=[rows(n) for n in widths],
        out_shape=[jax.ShapeDtypeStruct((M, n), BF16) for n in widths],
        compiler_params=_params("arbitrary"),
        name="inproj",
    )(x, ada, ada, *tables, *weights)


def _mla_attn_kernel(q_ref, k_ref, v_ref, o_ref, m_ref, l_ref, acc_ref):
    tq = tk = m_ref.shape[1]
    for g in range(q_ref.shape[0] // tq):
        _mla_sweep(q_ref.at[g * tq:(g + 1) * tq], k_ref, v_ref, o_ref.at[g * tq:(g + 1) * tq],
                   m_ref, l_ref, acc_ref, pl.program_id(2) * (q_ref.shape[0] // tq) + g)


def _mla_sweep(q_ref, k_ref, v_ref, o_ref, m_ref, l_ref, acc_ref, qi):
    tq = tk = q_ref.shape[0]
    heads = q_ref.shape[1] // MLA_QK_PAD
    reps = tk // LANES

    def step(c, first):
        kv_rows = pl.ds(pl.multiple_of(c * tk, tk), tk)
        scores = [lax.dot_general(q_ref[:, hd * MLA_QK_PAD:(hd + 1) * MLA_QK_PAD],
                                  k_ref[kv_rows, hd * MLA_QK_PAD:(hd + 1) * MLA_QK_PAD],
                                  (((1,), (1,)), ((), ())), preferred_element_type=F32)
                  for hd in range(heads)]
        probs, corrs = [], []
        for hd, s in enumerate(scores):
            if first:
                row = lax.broadcasted_iota(jnp.int32, (tq, tk), 0)
                col = lax.broadcasted_iota(jnp.int32, (tq, tk), 1)
                s = jnp.where(col <= row, s, -jnp.inf)
                m_new = jnp.broadcast_to(jnp.max(s, axis=-1, keepdims=True), (tq, LANES))
                p = jnp.exp2(s - jnp.tile(m_new, (1, reps)))
                l_ref[hd] = jnp.broadcast_to(jnp.sum(p, axis=-1, keepdims=True), (tq, LANES))
                corrs.append(None)
            else:
                m_prev = m_ref[hd]
                m_new = jnp.maximum(m_prev, jnp.max(s, axis=-1, keepdims=True))
                corr = jnp.exp2(m_prev - m_new)
                p = jnp.exp2(s - jnp.tile(m_new, (1, reps)))
                l_ref[hd] = l_ref[hd] * corr + jnp.sum(p, axis=-1, keepdims=True)
                corrs.append(corr)
            m_ref[hd] = m_new
            probs.append(p.astype(BF16))
        for hd, (p, corr) in enumerate(zip(probs, corrs)):
            pv = jnp.dot(p, v_ref[kv_rows, hd * HEAD_DIM:(hd + 1) * HEAD_DIM],
                         preferred_element_type=F32)
            acc_ref[hd] = pv if first else acc_ref[hd] * corr + pv

    step(qi, True)

    def body(c, carry):
        step(c, False)
        return carry

    lax.fori_loop(0, qi, body, 0)
    for hd in range(heads):
        o_ref[:, hd * HEAD_DIM:(hd + 1) * HEAD_DIM] = acc_ref[hd] / l_ref[hd]


def _mla_attention(q, k, v, *, batch, seq, t=512, heads_per_step=4, q_blocks_per_step=4):
    M = q.shape[0]
    t = min(t, seq)
    q_blocks_per_step = min(q_blocks_per_step, seq // t)
    tq = t * q_blocks_per_step
    nq = seq // tq
    k3 = k.reshape(batch, seq, k.shape[1])
    v3 = v.reshape(batch, seq, v.shape[1])
    qk_width = heads_per_step * MLA_QK_PAD
    v_width = heads_per_step * HEAD_DIM
    return pl.pallas_call(
        _mla_attn_kernel,
        scratch_shapes=[pltpu.VMEM((heads_per_step, t, LANES), F32),
                        pltpu.VMEM((heads_per_step, t, LANES), F32),
                        pltpu.VMEM((heads_per_step, t, HEAD_DIM), F32)],
        grid=(batch, MLA_HEADS // heads_per_step, nq),
        in_specs=[
            pl.BlockSpec((tq, qk_width), lambda b, h, i: (b * nq + i, h)),
            pl.BlockSpec((None, seq, qk_width), lambda b, h, i: (b, 0, h)),
            pl.BlockSpec((None, seq, v_width), lambda b, h, i: (b, 0, h)),
        ],
        out_specs=pl.BlockSpec((tq, v_width), lambda b, h, i: (b * nq + i, h)),
        out_shape=jax.ShapeDtypeStruct((M, MLA_HEADS * HEAD_DIM), F32),
        compiler_params=_params("arbitrary", "arbitrary", "arbitrary"),
        name="mla_attn",
    )(q, k3, v3)


def _sb_attn_kernel(q_ref, k_ref, v_ref, o_ref, run_ref, *, tq, tk, rq):
    for g in range(q_ref.shape[0] // tq):
        _sb_sweep(q_ref.at[g * tq:(g + 1) * tq], k_ref, v_ref, o_ref.at[g * tq:(g + 1) * tq],
                  run_ref, pl.program_id(2) * (q_ref.shape[0] // tq) + g, tk=tk, rq=rq)


def _sb_sweep(q_ref, k_ref, v_ref, o_ref, run_ref, qi, *, tk, rq):
    tq = q_ref.shape[0]
    heads = q_ref.shape[1] // HEAD_DIM
    blocks_per_step = tq // tk
    chunks = [(hd, r) for hd in range(heads) for r in range(tq // rq)]
    jj = lax.broadcasted_iota(jnp.int32, (tk, tk), 0)
    ss = lax.broadcasted_iota(jnp.int32, (tk, tk), 1)
    suffix = (jj > ss).astype(BF16)
    sign_bit = jnp.int32(-2 ** 31)

    def step(first, masked):
        row = lax.broadcasted_iota(jnp.int32, (rq, tk), 0)
        col = lax.broadcasted_iota(jnp.int32, (rq, tk), 1)
        tiles = []
        for d in reversed(range(blocks_per_step)):
            for u, (_, r) in enumerate(chunks):
                if masked and d * tk >= (r + 1) * rq - 1:
                    continue
                partial = masked and (d + 1) * tk - 1 >= r * rq
                tiles.append((u, d, (col + d * tk < row + r * rq) if partial else None))

        def kv_rows(d):
            return pl.ds(pl.multiple_of((first + d) * tk, tk), tk)

        def cols(u):
            hd = chunks[u][0]
            return slice(hd * HEAD_DIM, (hd + 1) * HEAD_DIM)

        def rows(u):
            r = chunks[u][1]
            return slice(r * rq, (r + 1) * rq)

        runs = [jnp.zeros((rq, LANES), F32) if masked else None for _ in chunks]
        started = [not masked for _ in chunks]
        n = len(tiles)
        zs, log_betas, tails = [None] * n, [None] * n, [None] * n

        def scores(i):
            u, d, _ = tiles[i]
            zs[i] = lax.dot_general(q_ref[rows(u), cols(u)], k_ref[kv_rows(d), cols(u)],
                                    (((1,), (1,)), ((), ())), preferred_element_type=F32)

        def log_terms(i):
            u, d, mask = tiles[i]
            z = zs[i]
            neg_abs = pltpu.bitcast(pltpu.bitcast(z, jnp.int32) | sign_bit, F32)
            softplus = jnp.log(1.0 + jnp.exp2(neg_abs)) * LOG2_E
            log_beta = jnp.minimum(z, 0.0) - softplus
            log_om = log_beta - z
            if mask is not None:
                log_om = jnp.where(mask, log_om, 0.0)
            if runs[u] is None:
                runs[u] = run_ref[u]
            log_betas[i] = log_beta + jnp.tile(runs[u], (1, tk // LANES))
            runs[u] = runs[u] + jnp.sum(log_om, axis=-1, keepdims=True)
            tails[i] = jnp.dot(log_om.astype(BF16), suffix, preferred_element_type=F32)

        def accumulate(i):
            u, d, mask = tiles[i]
            a = jnp.exp2(log_betas[i] + tails[i])
            if mask is not None:
                a = jnp.where(mask, a, 0.0)
            av = jnp.dot(a.astype(BF16), v_ref[kv_rows(d), cols(u)], preferred_element_type=F32)
            o_ref[rows(u), cols(u)] = o_ref[rows(u), cols(u)] + av if started[u] else av
            started[u] = True

        group, lag = 2, 1
        groups = [range(s, min(s + group, n)) for s in range(0, n, group)]
        for t in range(len(groups) + 1 + lag):
            for stage, back in ((scores, 0), (log_terms, 1), (accumulate, 1 + lag)):
                if 0 <= t - back < len(groups):
                    for i in groups[t - back]:
                        stage(i)
        for u in range(len(chunks)):
            if runs[u] is not None:
                run_ref[u] = runs[u]
            if not started[u]:
                o_ref[rows(u), cols(u)] = jnp.zeros((rq, HEAD_DIM), F32)
        return jnp.max(functools.reduce(
            jnp.maximum, [run_ref[u] if run is None else run for u, run in enumerate(runs)]))

    top = step(qi * blocks_per_step, True)

    def body(state):
        n, _ = state
        return n + 1, step((qi - 1 - n) * blocks_per_step, False)

    lax.while_loop(lambda state: (state[0] < qi) & (state[1] > SB_DEAD_LOG2), body,
                   (jnp.int32(0), top))


def _sb_attention(q, k, v, *, batch, seq, tq=256, tk=256, rq=256, heads_per_step=8,
                  q_blocks_per_step=4):
    M = q.shape[0]
    t = min(tq, seq)
    tk = min(tk, t)
    rq = min(rq, t)
    q_blocks_per_step = min(q_blocks_per_step, seq // t)
    rows = t * q_blocks_per_step
    nq = seq // rows
    width = heads_per_step * HEAD_DIM
    k3 = k.reshape(batch, seq, k.shape[1])
    v3 = v.reshape(batch, seq, v.shape[1])
    return pl.pallas_call(
        functools.partial(_sb_attn_kernel, tq=t, tk=tk, rq=rq),
        scratch_shapes=[pltpu.VMEM((heads_per_step * (t // rq), rq, LANES), F32)],
        grid=(batch, SB_HEADS // heads_per_step, nq),
        in_specs=[
            pl.BlockSpec((rows, width), lambda b, h, i: (b * nq + i, h)),
            pl.BlockSpec((None, seq, width), lambda b, h, i: (b, 0, h)),
            pl.BlockSpec((None, seq, width), lambda b, h, i: (b, 0, h)),
        ],
        out_specs=pl.BlockSpec((rows, width), lambda b, h, i: (b * nq + i, h)),
        out_shape=jax.ShapeDtypeStruct((M, SB_HEADS * HEAD_DIM), F32),
        compiler_params=_params("arbitrary", "arbitrary", "arbitrary"),
        name="sb_attn",
    )(q, k3, v3)


def _outproj_kernel(x_ref, om_ref, os_ref, gt_ref, mg_ref, sg_ref, wom_ref, wos_ref,
                    lng_ref, lnb_ref, o_ref, *, alpha):
    rows_total = x_ref.shape[0]
    chunk = min(OUTPROJ_ROW_CHUNK, rows_total)
    gate = (1.0 + gt_ref[...]) / alpha
    for r0 in range(0, rows_total, chunk):
        rows = slice(r0, r0 + chunk)
        om = _rms_norm(om_ref[rows, :], mg_ref[...]).astype(BF16)
        os_ = _rms_norm(os_ref[rows, :], sg_ref[...]).astype(BF16)
        m = (jnp.dot(om, wom_ref[...], preferred_element_type=F32)
             + jnp.dot(os_, wos_ref[...], preferred_element_type=F32))
        y = x_ref[rows, :] + gate * m
        o_ref[rows, :] = _layer_norm(y, lng_ref[...], lnb_ref[...], LN_EPS / (alpha * alpha))


def _outproj_block(x, o_mla, o_sb, ada, l, k_gate, w, ln_g, ln_b, k_ln, *, alpha,
                   rows_per_batch, tm=512):
    M, D = x.shape
    tm = min(tm, rows_per_batch)
    tiles_per_batch = rows_per_batch // tm

    def rows(width):
        return pl.BlockSpec((tm, width), lambda i: (i, 0))

    consts = [w["mla_out_g"], w["sb_out_g"], w["o_mla"], w["o_sb"]]
    ln_spec = pl.BlockSpec((None, None, 1, D), lambda i: (l, k_ln, 0, 0))
    return pl.pallas_call(
        functools.partial(_outproj_kernel, alpha=alpha),
        grid=(M // tm,),
        in_specs=[rows(D), rows(o_mla.shape[1]), rows(o_sb.shape[1]),
                  pl.BlockSpec((None, None, None, 1, D),
                               lambda i: (l, i // tiles_per_batch, k_gate, 0, 0))]
                 + [_layer_resident(a, l) for a in consts] + [ln_spec, ln_spec],
        out_specs=rows(D),
        out_shape=jax.ShapeDtypeStruct((M, D), F32),
        compiler_params=_params("arbitrary"),
        name="outproj",
    )(x, o_mla, o_sb, ada, *consts, ln_g, ln_b)


def _mixer_weights(w_in, q_norm_g, kv_norm_g, w_uq, w_ukv, mla_out_g, sb_out_g, w_o):
    depth = w_in.shape[0]
    o1 = Q_LORA_RANK
    o2 = o1 + KV_LORA_RANK
    o3 = o2 + QK_ROPE_DIM
    lane_pad = ((0, 0),) * 2 + ((0, LANES - QK_ROPE_DIM),)
    uq = w_uq.astype(BF16).reshape(depth, Q_LORA_RANK, MLA_HEADS, MLA_QK_DIM)
    uq_rope = jnp.pad(uq[..., QK_NOPE_DIM:], ((0, 0),) + lane_pad)
    ukv = w_ukv.astype(BF16).reshape(depth, KV_LORA_RANK, MLA_HEADS, QK_NOPE_DIM + HEAD_DIM)
    mla_w = MLA_HEADS * HEAD_DIM
    w_in = w_in.astype(BF16)
    return {
        "cq": w_in[:, :, :o1],
        "ckv": w_in[:, :, o1:o2],
        "kr": jnp.pad(w_in[:, :, o2:o3], lane_pad),
        "sb": w_in[:, :, o3:],
        "qn_g": q_norm_g.reshape(depth, 1, -1),
        "kvn_g": kv_norm_g.reshape(depth, 1, -1),
        "uq_nope": uq[..., :QK_NOPE_DIM].reshape(depth, Q_LORA_RANK, MLA_HEADS * QK_NOPE_DIM),
        "uq_rope": uq_rope.reshape(depth, Q_LORA_RANK, MLA_HEADS * LANES),
        "uk": ukv[..., :QK_NOPE_DIM].reshape(depth, KV_LORA_RANK, MLA_HEADS * QK_NOPE_DIM),
        "uv": ukv[..., QK_NOPE_DIM:].reshape(depth, KV_LORA_RANK, MLA_HEADS * HEAD_DIM),
        "mla_out_g": mla_out_g.reshape(depth, 1, -1),
        "sb_out_g": sb_out_g.reshape(depth, 1, -1),
        "o_mla": w_o[:, :mla_w].astype(BF16),
        "o_sb": w_o[:, mla_w:].astype(BF16),
    }


def kernel(x, c, positions, ada_w, ada_b, ln_g, ln_b, ffn1_wi, ffn1_wo, w_in, q_norm_g, kv_norm_g, w_uq, w_ukv, mla_out_g, sb_out_g, w_o, ffn2_wi, ffn2_wo):
    B, S, D = x.shape
    depth = ada_w.shape[0]
    alpha = (2.0 * depth) ** 0.25
    M = B * S

    ada = _ada_all_layers(c, ada_w, ada_b).reshape(depth, B, N_ADA, 1, D)
    tables = _rope_tables(positions)
    ln_g = ln_g.reshape(depth, 3, 1, D)
    ln_b = ln_b.reshape(depth, 3, 1, D)
    ffn1 = (ffn1_wi.astype(BF16), ffn1_wo.astype(BF16))
    ffn2 = (ffn2_wi.astype(BF16), ffn2_wo.astype(BF16))
    w = _mixer_weights(w_in, q_norm_g, kv_norm_g, w_uq, w_ukv, mla_out_g, sb_out_g, w_o)
    xf = x.reshape(M, D)
    for l in range(depth):
        xf = _ffn_block(xf, ada, l, 0, *ffn1, ln_g, ln_b, 0, alpha=alpha, rows_per_batch=S)
        q, k, v, sq, sk, sv = _inproj_block(xf, ada, l, 3, tables, w, rows_per_batch=S)
        o_mla = _mla_attention(q, k, v, batch=B, seq=S)
        o_sb = _sb_attention(sq, sk, sv, batch=B, seq=S)
        xf = _outproj_block(xf, o_mla, o_sb, ada, l, 5, w, ln_g, ln_b, 1,
                            alpha=alpha, rows_per_batch=S)
        xf = _ffn_block(xf, ada, l, 6, *ffn2, ln_g, ln_b, 2, alpha=alpha, rows_per_batch=S)
    return xf.reshape(B, S, D)
```

```python
import functools

import jax
import jax.numpy as jnp
from jax import lax
from jax.experimental import pallas as pl
from jax.experimental.pallas import tpu as pltpu

HEAD_DIM = 128
MLA_HEADS = 8
SB_HEADS = 8
Q_LORA_RANK = 768
KV_LORA_RANK = 512
QK_NOPE_DIM = 128
QK_ROPE_DIM = 64
MLA_QK_DIM = QK_NOPE_DIM + QK_ROPE_DIM
ROPE_THETA = 10000.0
N_ADA = 9
LN_EPS = 1e-5
RMS_EPS = 1e-6
FFN_RESIDUAL_WEIGHT = 0.5
LOG2_E = 1.4426950408889634
SB_DEAD_LOG2 = -152.0

LANES = 128
MLA_QK_PAD = 2 * LANES
VMEM_LIMIT_BYTES = 60 * 1024 * 1024
OUTPROJ_ROW_CHUNK = 256
FFN_EPILOGUE_ROW_CHUNK = 256

F32 = jnp.float32
BF16 = jnp.bfloat16


def _params(*semantics):
    return pltpu.CompilerParams(dimension_semantics=semantics,
                                vmem_limit_bytes=VMEM_LIMIT_BYTES)


def _layer_resident(stacked, l):
    index = (l,) + (0,) * (stacked.ndim - 1)
    return pl.BlockSpec((None,) + stacked.shape[1:], lambda *_: index,
                        pipeline_mode=pl.Buffered(1))


def _layer_norm(y, g, b, eps=LN_EPS):
    mu = jnp.mean(y, axis=-1, keepdims=True)
    d = y - mu
    var = jnp.mean(d * d, axis=-1, keepdims=True)
    return d * lax.rsqrt(var + eps) * g + b


def _rms_norm(y, g):
    return y * lax.rsqrt(jnp.mean(y * y, axis=-1, keepdims=True) + RMS_EPS) * g


def _ada_kernel(c_ref, w_ref, b_ref, o_ref):
    c = c_ref[...]
    c_act = (c * jax.nn.sigmoid(c)).astype(BF16)
    acc = jnp.dot(c_act, w_ref[...].astype(BF16), preferred_element_type=F32)
    o_ref[...] = acc + b_ref[...]


def _ada_all_layers(c, ada_w, ada_b, tn=2048):
    L, D, N = ada_w.shape
    B = c.shape[0]
    tn = min(tn, D)
    assert D % tn == 0
    return pl.pallas_call(
        _ada_kernel,
        grid=(L, N // tn),
        in_specs=[
            pl.BlockSpec((B, D), lambda l, j: (0, 0)),
            pl.BlockSpec((None, D, tn), lambda l, j: (l, 0, j)),
            pl.BlockSpec((None, 1, tn), lambda l, j: (l, 0, j)),
        ],
        out_specs=pl.BlockSpec((None, B, tn), lambda l, j: (l, 0, j)),
        out_shape=jax.ShapeDtypeStruct((L, B, N), F32),
        compiler_params=_params("arbitrary", "arbitrary"),
        name="ada",
    )(c, ada_w, ada_b.reshape(L, 1, N))


def _rope_table_kernel(pos_ref, invf_ref, cos_ref, sinlo_ref, sinhi_ref):
    half = QK_ROPE_DIM // 2
    ang = pos_ref[...].astype(F32) * invf_ref[...]
    lane = lax.broadcasted_iota(jnp.int32, ang.shape, 1)
    cos = jnp.cos(ang)
    sin = jnp.sin(ang)
    cos_ref[...] = jnp.where(lane < QK_ROPE_DIM, cos, 0.0)
    sinlo_ref[...] = jnp.where(lane < half, -sin, 0.0)
    sinhi_ref[...] = jnp.where((lane >= half) & (lane < QK_ROPE_DIM), sin, 0.0)


def _rope_tables(positions, tm=1024):
    M = positions.size
    half = QK_ROPE_DIM // 2
    inv_freq = ROPE_THETA ** (-jnp.arange(half, dtype=F32) / half)
    invf = jnp.tile(inv_freq, LANES // half).reshape(1, LANES)
    tm = min(tm, M)
    spec = pl.BlockSpec((tm, LANES), lambda i: (i, 0))
    shape = jax.ShapeDtypeStruct((M, LANES), F32)
    return pl.pallas_call(
        _rope_table_kernel,
        grid=(M // tm,),
        in_specs=[pl.BlockSpec((tm, 1), lambda i: (i, 0)),
                  pl.BlockSpec((1, LANES), lambda i: (0, 0))],
        out_specs=[spec, spec, spec],
        out_shape=[shape, shape, shape],
        compiler_params=_params("arbitrary"),
        name="rope_tables",
    )(positions.reshape(M, 1), invf)


def _ffn_kernel(x_ref, sh_ref, sc_ref, gt_ref, wg_ref, wu_ref, wo_ref, lng_ref, lnb_ref,
                o_ref, *, alpha, hidden_tiles):
    j = pl.program_id(1)
    last = hidden_tiles - 1

    def gated():
        h = (x_ref[...] * (1.0 + sc_ref[...]) + sh_ref[...]).astype(BF16)
        g = jnp.dot(h, wg_ref[...], preferred_element_type=F32)
        u = jnp.dot(h, wu_ref[...], preferred_element_type=F32)
        return (g * jax.nn.sigmoid(g) * u).astype(BF16)

    if hidden_tiles > 1:
        @pl.when(j == 0)
        def _():
            o_ref[...] = jnp.dot(gated(), wo_ref[...], preferred_element_type=F32)

        @pl.when((j > 0) & (j < last))
        def _():
            o_ref[...] += jnp.dot(gated(), wo_ref[...], preferred_element_type=F32)

    @pl.when(j == last)
    def _():
        a = gated()
        c = (FFN_RESIDUAL_WEIGHT / alpha) * (1.0 + gt_ref[...])
        chunk = min(FFN_EPILOGUE_ROW_CHUNK, a.shape[0])
        for r0 in range(0, a.shape[0], chunk):
            rows = slice(r0, r0 + chunk)
            f = jnp.dot(a[rows], wo_ref[...], preferred_element_type=F32)
            if hidden_tiles > 1:
                f = o_ref[rows, :] + f
            y = x_ref[rows, :] + c * f
            o_ref[rows, :] = _layer_norm(y, lng_ref[...], lnb_ref[...], LN_EPS / (alpha * alpha))


def _ffn_block(x, ada, l, k_shift, wi, wo, ln_g, ln_b, k_ln, *, alpha, rows_per_batch,
               tm=1024, tf=512):
    M, D = x.shape
    F = wo.shape[1]
    tm = min(tm, rows_per_batch)
    tf = min(tf, F)
    nf = F // tf
    tiles_per_batch = rows_per_batch // tm

    def ada_spec(k):
        return pl.BlockSpec((None, None, None, 1, D),
                            lambda i, j: (l, i // tiles_per_batch, k, 0, 0))

    ln_spec = pl.BlockSpec((None, None, 1, D), lambda i, j: (l, k_ln, 0, 0))
    return pl.pallas_call(
        functools.partial(_ffn_kernel, alpha=alpha, hidden_tiles=nf),
        grid=(M // tm, nf),
        in_specs=[
            pl.BlockSpec((tm, D), lambda i, j: (i, 0)),
            ada_spec(k_shift), ada_spec(k_shift + 1), ada_spec(k_shift + 2),
            pl.BlockSpec((None, D, tf), lambda i, j: (l, 0, j)),
            pl.BlockSpec((None, D, tf), lambda i, j: (l, 0, j + nf)),
            pl.BlockSpec((None, tf, D), lambda i, j: (l, j, 0)),
            ln_spec, ln_spec,
        ],
        out_specs=pl.BlockSpec((tm, D), lambda i, j: (i, 0)),
        out_shape=jax.ShapeDtypeStruct((M, D), F32),
        compiler_params=_params("arbitrary", "arbitrary"),
        name="ffn",
    )(x, ada, ada, ada, wi, wi, wo, ln_g, ln_b)


def _inproj_kernel(x_ref, sh_ref, sc_ref, cos_ref, sinlo_ref, sinhi_ref,
                   wcq_ref, wckv_ref, wkr_ref, wsb_ref, qng_ref, kvng_ref,
                   wuqn_ref, wuqr_ref, wuk_ref, wuv_ref,
                   q_ref, k_ref, v_ref, sq_ref, sk_ref, sv_ref):
    h = (x_ref[...] * (1.0 + sc_ref[...]) + sh_ref[...]).astype(BF16)
    cos, sin_lo, sin_hi = cos_ref[...], sinlo_ref[...], sinhi_ref[...]
    half = QK_ROPE_DIM // 2

    def rope(xr):
        return (xr * cos + pltpu.roll(xr, LANES - half, axis=1) * sin_lo
                + pltpu.roll(xr, half, axis=1) * sin_hi)

    c_q = jnp.dot(h, wcq_ref[...], preferred_element_type=F32)
    c_kv = jnp.dot(h, wckv_ref[...], preferred_element_type=F32)
    k_rope = jnp.dot(h, wkr_ref[...], preferred_element_type=F32)

    sb_w = SB_HEADS * HEAD_DIM
    sb = jnp.dot(h, wsb_ref[...], preferred_element_type=F32)
    sq_ref[...] = (sb[:, :sb_w] * (LOG2_E * HEAD_DIM ** -0.5)).astype(BF16)
    sk_ref[...] = sb[:, sb_w:2 * sb_w].astype(BF16)
    sv_ref[...] = sb[:, 2 * sb_w:].astype(BF16)

    c_qn = _rms_norm(c_q, qng_ref[...]).astype(BF16)
    c_kvn = _rms_norm(c_kv, kvng_ref[...]).astype(BF16)
    q_nope = jnp.dot(c_qn, wuqn_ref[...], preferred_element_type=F32)
    q_rope = jnp.dot(c_qn, wuqr_ref[...], preferred_element_type=F32)
    k_nope = jnp.dot(c_kvn, wuk_ref[...], preferred_element_type=F32)
    v_ref[...] = jnp.dot(c_kvn, wuv_ref[...], preferred_element_type=F32).astype(BF16)
    q_scale = LOG2_E * MLA_QK_DIM ** -0.5
    for hd in range(MLA_HEADS):
        lo = hd * MLA_QK_PAD
        q_ref[:, lo:lo + LANES] = (q_nope[:, hd * LANES:(hd + 1) * LANES] * q_scale).astype(BF16)
        q_ref[:, lo + LANES:lo + 2 * LANES] = (
            rope(q_rope[:, hd * LANES:(hd + 1) * LANES]) * q_scale).astype(BF16)

    k_pe = rope(k_rope).astype(BF16)
    for hd in range(MLA_HEADS):
        lo = hd * MLA_QK_PAD
        k_ref[:, lo:lo + LANES] = k_nope[:, hd * LANES:(hd + 1) * LANES].astype(BF16)
        k_ref[:, lo + LANES:lo + 2 * LANES] = k_pe


def _inproj_block(x, ada, l, k_shift, tables, w, *, rows_per_batch, tm=512):
    M, D = x.shape
    tm = min(tm, rows_per_batch)
    tiles_per_batch = rows_per_batch // tm

    def ada_spec(k):
        return pl.BlockSpec((None, None, None, 1, D),
                            lambda i: (l, i // tiles_per_batch, k, 0, 0))

    def rows(width):
        return pl.BlockSpec((tm, width), lambda i: (i, 0))

    weights = [w["cq"], w["ckv"], w["kr"], w["sb"], w["qn_g"], w["kvn_g"],
               w["uq_nope"], w["uq_rope"], w["uk"], w["uv"]]
    widths = [MLA_HEADS * MLA_QK_PAD, MLA_HEADS * MLA_QK_PAD, MLA_HEADS * HEAD_DIM,
              SB_HEADS * HEAD_DIM, SB_HEADS * HEAD_DIM, SB_HEADS * HEAD_DIM]
    return pl.pallas_call(
        _inproj_kernel,
        grid=(M // tm,),
        in_specs=[rows(D), ada_spec(k_shift), ada_spec(k_shift + 1),
                  rows(LANES), rows(LANES), rows(LANES)]
                 + [_layer_resident(a, l) for a in weights],
        out_specs=[rows(n) for n in widths],
        out_shape=[jax.ShapeDtypeStruct((M, n), BF16) for n in widths],
        compiler_params=_params("arbitrary"),
        name="inproj",
    )(x, ada, ada, *tables, *weights)


def _mla_attn_kernel(q_ref, k_ref, v_ref, o_ref, m_ref, l_ref, acc_ref):
    tq = tk = m_ref.shape[1]
    for g in range(q_ref.shape[0] // tq):
        _mla_sweep(q_ref.at[g * tq:(g + 1) * tq], k_ref, v_ref, o_ref.at[g * tq:(g + 1) * tq],
                   m_ref, l_ref, acc_ref, pl.program_id(2) * (q_ref.shape[0] // tq) + g)


def _mla_sweep(q_ref, k_ref, v_ref, o_ref, m_ref, l_ref, acc_ref, qi):
    tq = tk = q_ref.shape[0]
    heads = q_ref.shape[1] // MLA_QK_PAD
    reps = tk // LANES

    def step(c, first):
        kv_rows = pl.ds(pl.multiple_of(c * tk, tk), tk)
        scores = [lax.dot_general(q_ref[:, hd * MLA_QK_PAD:(hd + 1) * MLA_QK_PAD],
                                  k_ref[kv_rows, hd * MLA_QK_PAD:(hd + 1) * MLA_QK_PAD],
                                  (((1,), (1,)), ((), ())), preferred_element_type=F32)
                  for hd in range(heads)]
        probs, corrs = [], []
        for hd, s in enumerate(scores):
            if first:
                row = lax.broadcasted_iota(jnp.int32, (tq, tk), 0)
                col = lax.broadcasted_iota(jnp.int32, (tq, tk), 1)
                s = jnp.where(col <= row, s, -jnp.inf)
                m_new = jnp.broadcast_to(jnp.max(s, axis=-1, keepdims=True), (tq, LANES))
                p = jnp.exp2(s - jnp.tile(m_new, (1, reps)))
                l_ref[hd] = jnp.broadcast_to(jnp.sum(p, axis=-1, keepdims=True), (tq, LANES))
                corrs.append(None)
            else:
                m_prev = m_ref[hd]
                m_new = jnp.maximum(m_prev, jnp.max(s, axis=-1, keepdims=True))
                corr = jnp.exp2(m_prev - m_new)
                p = jnp.exp2(s - jnp.tile(m_new, (1, reps)))
                l_ref[hd] = l_ref[hd] * corr + jnp.sum(p, axis=-1, keepdims=True)
                corrs.append(corr)
            m_ref[hd] = m_new
            probs.append(p.astype(BF16))
        for hd, (p, corr) in enumerate(zip(probs, corrs)):
            pv = jnp.dot(p, v_ref[kv_rows, hd * HEAD_DIM:(hd + 1) * HEAD_DIM],
                         preferred_element_type=F32)
            acc_ref[hd] = pv if first else acc_ref[hd] * corr + pv

    step(qi, True)

    def body(c, carry):
        step(c, False)
        return carry

    lax.fori_loop(0, qi, body, 0)
    for hd in range(heads):
        o_ref[:, hd * HEAD_DIM:(hd + 1) * HEAD_DIM] = acc_ref[hd] / l_ref[hd]


def _mla_attention(q, k, v, *, batch, seq, t=512, heads_per_step=4, q_blocks_per_step=4):
    M = q.shape[0]
    t = min(t, seq)
    q_blocks_per_step = min(q_blocks_per_step, seq // t)
    tq = t * q_blocks_per_step
    nq = seq // tq
    k3 = k.reshape(batch, seq, k.shape[1])
    v3 = v.reshape(batch, seq, v.shape[1])
    qk_width = heads_per_step * MLA_QK_PAD
    v_width = heads_per_step * HEAD_DIM
    return pl.pallas_call(
        _mla_attn_kernel,
        scratch_shapes=[pltpu.VMEM((heads_per_step, t, LANES), F32),
                        pltpu.VMEM((heads_per_step, t, LANES), F32),
                        pltpu.VMEM((heads_per_step, t, HEAD_DIM), F32)],
        grid=(batch, MLA_HEADS // heads_per_step, nq),
        in_specs=[
            pl.BlockSpec((tq, qk_width), lambda b, h, i: (b * nq + i, h)),
            pl.BlockSpec((None, seq, qk_width), lambda b, h, i: (b, 0, h)),
            pl.BlockSpec((None, seq, v_width), lambda b, h, i: (b, 0, h)),
        ],
        out_specs=pl.BlockSpec((tq, v_width), lambda b, h, i: (b * nq + i, h)),
        out_shape=jax.ShapeDtypeStruct((M, MLA_HEADS * HEAD_DIM), F32),
        compiler_params=_params("arbitrary", "arbitrary", "arbitrary"),
        name="mla_attn",
    )(q, k3, v3)


def _sb_attn_kernel(q_ref, k_ref, v_ref, o_ref, run_ref, *, tq, tk, rq):
    for g in range(q_ref.shape[0] // tq):
        _sb_sweep(q_ref.at[g * tq:(g + 1) * tq], k_ref, v_ref, o_ref.at[g * tq:(g + 1) * tq],
                  run_ref, pl.program_id(2) * (q_ref.shape[0] // tq) + g, tk=tk, rq=rq)


def _sb_sweep(q_ref, k_ref, v_ref, o_ref, run_ref, qi, *, tk, rq):
    tq = q_ref.shape[0]
    heads = q_ref.shape[1] // HEAD_DIM
    blocks_per_step = tq // tk
    chunks = [(hd, r) for hd in range(heads) for r in range(tq // rq)]
    jj = lax.broadcasted_iota(jnp.int32, (tk, tk), 0)
    ss = lax.broadcasted_iota(jnp.int32, (tk, tk), 1)
    suffix = (jj > ss).astype(BF16)
    sign_bit = jnp.int32(-2 ** 31)

    def step(first, masked):
        row = lax.broadcasted_iota(jnp.int32, (rq, tk), 0)
        col = lax.broadcasted_iota(jnp.int32, (rq, tk), 1)
        tiles = []
        for d in reversed(range(blocks_per_step)):
            for u, (_, r) in enumerate(chunks):
                if masked and d * tk >= (r + 1) * rq - 1:
                    continue
                partial = masked and (d + 1) * tk - 1 >= r * rq
                tiles.append((u, d, (col + d * tk < row + r * rq) if partial else None))

        def kv_rows(d):
            return pl.ds(pl.multiple_of((first + d) * tk, tk), tk)

        def cols(u):
            hd = chunks[u][0]
            return slice(hd * HEAD_DIM, (hd + 1) * HEAD_DIM)

        def rows(u):
            r = chunks[u][1]
            return slice(r * rq, (r + 1) * rq)

        runs = [jnp.zeros((rq, LANES), F32) if masked else None for _ in chunks]
        started = [not masked for _ in chunks]
        n = len(tiles)
        zs, log_betas, tails = [None] * n, [None] * n, [None] * n

        def scores(i):
            u, d, _ = tiles[i]
            zs[i] = lax.dot_general(q_ref[rows(u), cols(u)], k_ref[kv_rows(d), cols(u)],
                                    (((1,), (1,)), ((), ())), preferred_element_type=F32)

        def log_terms(i):
            u, d, mask = tiles[i]
            z = zs[i]
            neg_abs = pltpu.bitcast(pltpu.bitcast(z, jnp.int32) | sign_bit, F32)
            softplus = jnp.log(1.0 + jnp.exp2(neg_abs)) * LOG2_E
            log_beta = jnp.minimum(z, 0.0) - softplus
            log_om = log_beta - z
            if mask is not None:
                log_om = jnp.where(mask, log_om, 0.0)
            if runs[u] is None:
                runs[u] = run_ref[u]
            log_betas[i] = log_beta + jnp.tile(runs[u], (1, tk // LANES))
            runs[u] = runs[u] + jnp.sum(log_om, axis=-1, keepdims=True)
            tails[i] = jnp.dot(log_om.astype(BF16), suffix, preferred_element_type=F32)

        def accumulate(i):
            u, d, mask = tiles[i]
            a = jnp.exp2(log_betas[i] + tails[i])
            if mask is not None:
                a = jnp.where(mask, a, 0.0)
            av = jnp.dot(a.astype(BF16), v_ref[kv_rows(d), cols(u)], preferred_element_type=F32)
            o_ref[rows(u), cols(u)] = o_ref[rows(u), cols(u)] + av if started[u] else av
            started[u] = True

        group, lag = 2, 1
        groups = [range(s, min(s + group, n)) for s in range(0, n, group)]
        for t in range(len(groups) + 1 + lag):
            for stage, back in ((scores, 0), (log_terms, 1), (accumulate, 1 + lag)):
                if 0 <= t - back < len(groups):
                    for i in groups[t - back]:
                        stage(i)
        for u in range(len(chunks)):
            if runs[u] is not None:
                run_ref[u] = runs[u]
            if not started[u]:
                o_ref[rows(u), cols(u)] = jnp.zeros((rq, HEAD_DIM), F32)
        return jnp.max(functools.reduce(
            jnp.maximum, [run_ref[u] if run is None else run for u, run in enumerate(runs)]))

    top = step(qi * blocks_per_step, True)

    def body(state):
        n, _ = state
        return n + 1, step((qi - 1 - n) * blocks_per_step, False)

    lax.while_loop(lambda state: (state[0] < qi) & (state[1] > SB_DEAD_LOG2), body,
                   (jnp.int32(0), top))


def _sb_attention(q, k, v, *, batch, seq, tq=256, tk=256, rq=256, heads_per_step=8,
                  q_blocks_per_step=4):
    M = q.shape[0]
    t = min(tq, seq)
    tk = min(tk, t)
    rq = min(rq, t)
    q_blocks_per_step = min(q_blocks_per_step, seq // t)
    rows = t * q_blocks_per_step
    nq = seq // rows
    width = heads_per_step * HEAD_DIM
    k3 = k.reshape(batch, seq, k.shape[1])
    v3 = v.reshape(batch, seq, v.shape[1])
    return pl.pallas_call(
        functools.partial(_sb_attn_kernel, tq=t, tk=tk, rq=rq),
        scratch_shapes=[pltpu.VMEM((heads_per_step * (t // rq), rq, LANES), F32)],
        grid=(batch, SB_HEADS // heads_per_step, nq),
        in_specs=[
            pl.BlockSpec((rows, width), lambda b, h, i: (b * nq + i, h)),
            pl.BlockSpec((None, seq, width), lambda b, h, i: (b, 0, h)),
            pl.BlockSpec((None, seq, width), lambda b, h, i: (b, 0, h)),
        ],
        out_specs=pl.BlockSpec((rows, width), lambda b, h, i: (b * nq + i, h)),
        out_shape=jax.ShapeDtypeStruct((M, SB_HEADS * HEAD_DIM), F32),
        compiler_params=_params("arbitrary", "arbitrary", "arbitrary"),
        name="sb_attn",
    )(q, k3, v3)


def _outproj_kernel(x_ref, om_ref, os_ref, gt_ref, mg_ref, sg_ref, wom_ref, wos_ref,
                    lng_ref, lnb_ref, o_ref, *, alpha):
    rows_total = x_ref.shape[0]
    chunk = min(OUTPROJ_ROW_CHUNK, rows_total)
    gate = (1.0 + gt_ref[...]) / alpha
    for r0 in range(0, rows_total, chunk):
        rows = slice(r0, r0 + chunk)
        om = _rms_norm(om_ref[rows, :], mg_ref[...]).astype(BF16)
        os_ = _rms_norm(os_ref[rows, :], sg_ref[...]).astype(BF16)
        m = (jnp.dot(om, wom_ref[...], preferred_element_type=F32)
             + jnp.dot(os_, wos_ref[...], preferred_element_type=F32))
        y = x_ref[rows, :] + gate * m
        o_ref[rows, :] = _layer_norm(y, lng_ref[...], lnb_ref[...], LN_EPS / (alpha * alpha))


def _outproj_block(x, o_mla, o_sb, ada, l, k_gate, w, ln_g, ln_b, k_ln, *, alpha,
                   rows_per_batch, tm=512):
    M, D = x.shape
    tm = min(tm, rows_per_batch)
    tiles_per_batch = rows_per_batch // tm

    def rows(width):
        return pl.BlockSpec((tm, width), lambda i: (i, 0))

    consts = [w["mla_out_g"], w["sb_out_g"], w["o_mla"], w["o_sb"]]
    ln_spec = pl.BlockSpec((None, None, 1, D), lambda i: (l, k_ln, 0, 0))
    return pl.pallas_call(
        functools.partial(_outproj_kernel, alpha=alpha),
        grid=(M // tm,),
        in_specs=[rows(D), rows(o_mla.shape[1]), rows(o_sb.shape[1]),
                  pl.BlockSpec((None, None, None, 1, D),
                               lambda i: (l, i // tiles_per_batch, k_gate, 0, 0))]
                 + [_layer_resident(a, l) for a in consts] + [ln_spec, ln_spec],
        out_specs=rows(D),
        out_shape=jax.ShapeDtypeStruct((M, D), F32),
        compiler_params=_params("arbitrary"),
        name="outproj",
    )(x, o_mla, o_sb, ada, *consts, ln_g, ln_b)


def _mixer_weights(w_in, q_norm_g, kv_norm_g, w_uq, w_ukv, mla_out_g, sb_out_g, w_o):
    depth = w_in.shape[0]
    o1 = Q_LORA_RANK
    o2 = o1 + KV_LORA_RANK
    o3 = o2 + QK_ROPE_DIM
    lane_pad = ((0, 0),) * 2 + ((0, LANES - QK_ROPE_DIM),)
    uq = w_uq.astype(BF16).reshape(depth, Q_LORA_RANK, MLA_HEADS, MLA_QK_DIM)
    uq_rope = jnp.pad(uq[..., QK_NOPE_DIM:], ((0, 0),) + lane_pad)
    ukv = w_ukv.astype(BF16).reshape(depth, KV_LORA_RANK, MLA_HEADS, QK_NOPE_DIM + HEAD_DIM)
    mla_w = MLA_HEADS * HEAD_DIM
    w_in = w_in.astype(BF16)
    return {
        "cq": w_in[:, :, :o1],
        "ckv": w_in[:, :, o1:o2],
        "kr": jnp.pad(w_in[:, :, o2:o3], lane_pad),
        "sb": w_in[:, :, o3:],
        "qn_g": q_norm_g.reshape(depth, 1, -1),
        "kvn_g": kv_norm_g.reshape(depth, 1, -1),
        "uq_nope": uq[..., :QK_NOPE_DIM].reshape(depth, Q_LORA_RANK, MLA_HEADS * QK_NOPE_DIM),
        "uq_rope": uq_rope.reshape(depth, Q_LORA_RANK, MLA_HEADS * LANES),
        "uk": ukv[..., :QK_NOPE_DIM].reshape(depth, KV_LORA_RANK, MLA_HEADS * QK_NOPE_DIM),
        "uv": ukv[..., QK_NOPE_DIM:].reshape(depth, KV_LORA_RANK, MLA_HEADS * HEAD_DIM),
        "mla_out_g": mla_out_g.reshape(depth, 1, -1),
        "sb_out_g": sb_out_g.reshape(depth, 1, -1),
        "o_mla": w_o[:, :mla_w].astype(BF16),
        "o_sb": w_o[:, mla_w:].astype(BF16),
    }


def kernel(x, c, positions, ada_w, ada_b, ln_g, ln_b, ffn1_wi, ffn1_wo, w_in, q_norm_g, kv_norm_g, w_uq, w_ukv, mla_out_g, sb_out_g, w_o, ffn2_wi, ffn2_wo):
    B, S, D = x.shape
    depth = ada_w.shape[0]
    alpha = (2.0 * depth) ** 0.25
    M = B * S

    ada = _ada_all_layers(c, ada_w, ada_b).reshape(depth, B, N_ADA, 1, D)
    tables = _rope_tables(positions)
    ln_g = ln_g.reshape(depth, 3, 1, D)
    ln_b = ln_b.reshape(depth, 3, 1, D)
    ffn1 = (ffn1_wi.astype(BF16), ffn1_wo.astype(BF16))
    ffn2 = (ffn2_wi.astype(BF16), ffn2_wo.astype(BF16))
    w = _mixer_weights(w_in, q_norm_g, kv_norm_g, w_uq, w_ukv, mla_out_g, sb_out_g, w_o)
    xf = x.reshape(M, D)
    for l in range(depth):
        xf = _ffn_block(xf, ada, l, 0, *ffn1, ln_g, ln_b, 0, alpha=alpha, rows_per_batch=S)
        q, k, v, sq, sk, sv = _inproj_block(xf, ada, l, 3, tables, w, rows_per_batch=S)
        o_mla = _mla_attention(q, k, v, batch=B, seq=S)
        o_sb = _sb_attention(sq, sk, sv, batch=B, seq=S)
        xf = _outproj_block(xf, o_mla, o_sb, ada, l, 5, w, ln_g, ln_b, 1,
                            alpha=alpha, rows_per_batch=S)
        xf = _ffn_block(xf, ada, l, 6, *ffn2, ln_g, ln_b, 2, alpha=alpha, rows_per_batch=S)
    return xf.reshape(B, S, D)
```

```python
import functools

import jax
import jax.numpy as jnp
from jax import lax
from jax.experimental import pallas as pl
from jax.experimental.pallas import tpu as pltpu

HEAD_DIM = 128
MLA_HEADS = 8
SB_HEADS = 8
Q_LORA_RANK = 768
KV_LORA_RANK = 512
QK_NOPE_DIM = 128
QK_ROPE_DIM = 64
MLA_QK_DIM = QK_NOPE_DIM + QK_ROPE_DIM
ROPE_THETA = 10000.0
N_ADA = 9
LN_EPS = 1e-5
RMS_EPS = 1e-6
FFN_RESIDUAL_WEIGHT = 0.5
LOG2_E = 1.4426950408889634
SB_DEAD_LOG2 = -152.0

LANES = 128
MLA_QK_PAD = 2 * LANES
VMEM_LIMIT_BYTES = 60 * 1024 * 1024
OUTPROJ_ROW_CHUNK = 256
FFN_EPILOGUE_ROW_CHUNK = 256

F32 = jnp.float32
BF16 = jnp.bfloat16


def _params(*semantics):
    return pltpu.CompilerParams(dimension_semantics=semantics,
                                vmem_limit_bytes=VMEM_LIMIT_BYTES)


def _layer_resident(stacked, l):
    index = (l,) + (0,) * (stacked.ndim - 1)
    return pl.BlockSpec((None,) + stacked.shape[1:], lambda *_: index,
                        pipeline_mode=pl.Buffered(1))


def _layer_norm(y, g, b, eps=LN_EPS):
    mu = jnp.mean(y, axis=-1, keepdims=True)
    d = y - mu
    var = jnp.mean(d * d, axis=-1, keepdims=True)
    return d * lax.rsqrt(var + eps) * g + b


def _rms_norm(y, g):
    return y * lax.rsqrt(jnp.mean(y * y, axis=-1, keepdims=True) + RMS_EPS) * g


def _ada_kernel(c_ref, w_ref, b_ref, o_ref):
    c = c_ref[...]
    c_act = (c * jax.nn.sigmoid(c)).astype(BF16)
    acc = jnp.dot(c_act, w_ref[...].astype(BF16), preferred_element_type=F32)
    o_ref[...] = acc + b_ref[...]


def _ada_all_layers(c, ada_w, ada_b, tn=2048):
    L, D, N = ada_w.shape
    B = c.shape[0]
    tn = min(tn, D)
    assert D % tn == 0
    return pl.pallas_call(
        _ada_kernel,
        grid=(L, N // tn),
        in_specs=[
            pl.BlockSpec((B, D), lambda l, j: (0, 0)),
            pl.BlockSpec((None, D, tn), lambda l, j: (l, 0, j)),
            pl.BlockSpec((None, 1, tn), lambda l, j: (l, 0, j)),
        ],
        out_specs=pl.BlockSpec((None, B, tn), lambda l, j: (l, 0, j)),
        out_shape=jax.ShapeDtypeStruct((L, B, N), F32),
        compiler_params=_params("arbitrary", "arbitrary"),
        name="ada",
    )(c, ada_w, ada_b.reshape(L, 1, N))


def _rope_table_kernel(pos_ref, invf_ref, cos_ref, sinlo_ref, sinhi_ref):
    half = QK_ROPE_DIM // 2
    ang = pos_ref[...].astype(F32) * invf_ref[...]
    lane = lax.broadcasted_iota(jnp.int32, ang.shape, 1)
    cos = jnp.cos(ang)
    sin = jnp.sin(ang)
    cos_ref[...] = jnp.where(lane < QK_ROPE_DIM, cos, 0.0)
    sinlo_ref[...] = jnp.where(lane < half, -sin, 0.0)
    sinhi_ref[...] = jnp.where((lane >= half) & (lane < QK_ROPE_DIM), sin, 0.0)


def _rope_tables(positions, tm=1024):
    M = positions.size
    half = QK_ROPE_DIM // 2
    inv_freq = ROPE_THETA ** (-jnp.arange(half, dtype=F32) / half)
    invf = jnp.tile(inv_freq, LANES // half).reshape(1, LANES)
    tm = min(tm, M)
    spec = pl.BlockSpec((tm, LANES), lambda i: (i, 0))
    shape = jax.ShapeDtypeStruct((M, LANES), F32)
    return pl.pallas_call(
        _rope_table_kernel,
        grid=(M // tm,),
        in_specs=[pl.BlockSpec((tm, 1), lambda i: (i, 0)),
                  pl.BlockSpec((1, LANES), lambda i: (0, 0))],
        out_specs=[spec, spec, spec],
        out_shape=[shape, shape, shape],
        compiler_params=_params("arbitrary"),
        name="rope_tables",
    )(positions.reshape(M, 1), invf)


def _ffn_kernel(x_ref, sh_ref, sc_ref, gt_ref, wg_ref, wu_ref, wo_ref, lng_ref, lnb_ref,
                o_ref, *, alpha, hidden_tiles):
    j = pl.program_id(1)
    last = hidden_tiles - 1

    def gated():
        h = (x_ref[...] * (1.0 + sc_ref[...]) + sh_ref[...]).astype(BF16)
        g = jnp.dot(h, wg_ref[...], preferred_element_type=F32)
        u = jnp.dot(h, wu_ref[...], preferred_element_type=F32)
        return (g * jax.nn.sigmoid(g) * u).astype(BF16)

    if hidden_tiles > 1:
        @pl.when(j == 0)
        def _():
            o_ref[...] = jnp.dot(gated(), wo_ref[...], preferred_element_type=F32)

        @pl.when((j > 0) & (j < last))
        def _():
            o_ref[...] += jnp.dot(gated(), wo_ref[...], preferred_element_type=F32)

    @pl.when(j == last)
    def _():
        a = gated()
        c = (FFN_RESIDUAL_WEIGHT / alpha) * (1.0 + gt_ref[...])
        chunk = min(FFN_EPILOGUE_ROW_CHUNK, a.shape[0])
        for r0 in range(0, a.shape[0], chunk):
            rows = slice(r0, r0 + chunk)
            f = jnp.dot(a[rows], wo_ref[...], preferred_element_type=F32)
            if hidden_tiles > 1:
                f = o_ref[rows, :] + f
            y = x_ref[rows, :] + c * f
            o_ref[rows, :] = _layer_norm(y, lng_ref[...], lnb_ref[...], LN_EPS / (alpha * alpha))


def _ffn_block(x, ada, l, k_shift, wi, wo, ln_g, ln_b, k_ln, *, alpha, rows_per_batch,
               tm=1024, tf=512):
    M, D = x.shape
    F = wo.shape[1]
    tm = min(tm, rows_per_batch)
    tf = min(tf, F)
    nf = F // tf
    tiles_per_batch = rows_per_batch // tm

    def ada_spec(k):
        return pl.BlockSpec((None, None, None, 1, D),
                            lambda i, j: (l, i // tiles_per_batch, k, 0, 0))

    ln_spec = pl.BlockSpec((None, None, 1, D), lambda i, j: (l, k_ln, 0, 0))
    return pl.pallas_call(
        functools.partial(_ffn_kernel, alpha=alpha, hidden_tiles=nf),
        grid=(M // tm, nf),
        in_specs=[
            pl.BlockSpec((tm, D), lambda i, j: (i, 0)),
            ada_spec(k_shift), ada_spec(k_shift + 1), ada_spec(k_shift + 2),
            pl.BlockSpec((None, D, tf), lambda i, j: (l, 0, j)),
            pl.BlockSpec((None, D, tf), lambda i, j: (l, 0, j + nf)),
            pl.BlockSpec((None, tf, D), lambda i, j: (l, j, 0)),
            ln_spec, ln_spec,
        ],
        out_specs=pl.BlockSpec((tm, D), lambda i, j: (i, 0)),
        out_shape=jax.ShapeDtypeStruct((M, D), F32),
        compiler_params=_params("arbitrary", "arbitrary"),
        name="ffn",
    )(x, ada, ada, ada, wi, wi, wo, ln_g, ln_b)


def _inproj_kernel(x_ref, sh_ref, sc_ref, cos_ref, sinlo_ref, sinhi_ref,
                   wcq_ref, wckv_ref, wkr_ref, wsb_ref, qng_ref, kvng_ref,
                   wuqn_ref, wuqr_ref, wuk_ref, wuv_ref,
                   q_ref, k_ref, v_ref, sq_ref, sk_ref, sv_ref):
    h = (x_ref[...] * (1.0 + sc_ref[...]) + sh_ref[...]).astype(BF16)
    cos, sin_lo, sin_hi = cos_ref[...], sinlo_ref[...], sinhi_ref[...]
    half = QK_ROPE_DIM // 2

    def rope(xr):
        return (xr * cos + pltpu.roll(xr, LANES - half, axis=1) * sin_lo
                + pltpu.roll(xr, half, axis=1) * sin_hi)

    c_q = jnp.dot(h, wcq_ref[...], preferred_element_type=F32)
    c_kv = jnp.dot(h, wckv_ref[...], preferred_element_type=F32)
    k_rope = jnp.dot(h, wkr_ref[...], preferred_element_type=F32)

    sb_w = SB_HEADS * HEAD_DIM
    sb = jnp.dot(h, wsb_ref[...], preferred_element_type=F32)
    sq_ref[...] = (sb[:, :sb_w] * (LOG2_E * HEAD_DIM ** -0.5)).astype(BF16)
    sk_ref[...] = sb[:, sb_w:2 * sb_w].astype(BF16)
    sv_ref[...] = sb[:, 2 * sb_w:].astype(BF16)

    c_qn = _rms_norm(c_q, qng_ref[...]).astype(BF16)
    c_kvn = _rms_norm(c_kv, kvng_ref[...]).astype(BF16)
    q_nope = jnp.dot(c_qn, wuqn_ref[...], preferred_element_type=F32)
    q_rope = jnp.dot(c_qn, wuqr_ref[...], preferred_element_type=F32)
    k_nope = jnp.dot(c_kvn, wuk_ref[...], preferred_element_type=F32)
    v_ref[...] = jnp.dot(c_kvn, wuv_ref[...], preferred_element_type=F32).astype(BF16)
    q_scale = LOG2_E * MLA_QK_DIM ** -0.5
    for hd in range(MLA_HEADS):
        lo = hd * MLA_QK_PAD
        q_ref[:, lo:lo + LANES] = (q_nope[:, hd * LANES:(hd + 1) * LANES] * q_scale).astype(BF16)
        q_ref[:, lo + LANES:lo + 2 * LANES] = (
            rope(q_rope[:, hd * LANES:(hd + 1) * LANES]) * q_scale).astype(BF16)

    k_pe = rope(k_rope).astype(BF16)
    for hd in range(MLA_HEADS):
        lo = hd * MLA_QK_PAD
        k_ref[:, lo:lo + LANES] = k_nope[:, hd * LANES:(hd + 1) * LANES].astype(BF16)
        k_ref[:, lo + LANES:lo + 2 * LANES] = k_pe


def _inproj_block(x, ada, l, k_shift, tables, w, *, rows_per_batch, tm=512):
    M, D = x.shape
    tm = min(tm, rows_per_batch)
    tiles_per_batch = rows_per_batch // tm

    def ada_spec(k):
        return pl.BlockSpec((None, None, None, 1, D),
                            lambda i: (l, i // tiles_per_batch, k, 0, 0))

    def rows(width):
        return pl.BlockSpec((tm, width), lambda i: (i, 0))

    weights = [w["cq"], w["ckv"], w["kr"], w["sb"], w["qn_g"], w["kvn_g"],
               w["uq_nope"], w["uq_rope"], w["uk"], w["uv"]]
    widths = [MLA_HEADS * MLA_QK_PAD, MLA_HEADS * MLA_QK_PAD, MLA_HEADS * HEAD_DIM,
              SB_HEADS * HEAD_DIM, SB_HEADS * HEAD_DIM, SB_HEADS * HEAD_DIM]
    return pl.pallas_call(
        _inproj_kernel,
        grid=(M // tm,),
        in_specs=[rows(D), ada_spec(k_shift), ada_spec(k_shift + 1),
                  rows(LANES), rows(LANES), rows(LANES)]
                 + [_layer_resident(a, l) for a in weights],
        out_specs=[rows(n) for n in widths],
        out_shape=[jax.ShapeDtypeStruct((M, n), BF16) for n in widths],
        compiler_params=_params("arbitrary"),
        name="inproj",
    )(x, ada, ada, *tables, *weights)


def _mla_attn_kernel(q_ref, k_ref, v_ref, o_ref, m_ref, l_ref, acc_ref):
    tq = tk = m_ref.shape[1]
    for g in range(q_ref.shape[0] // tq):
        _mla_sweep(q_ref.at[g * tq:(g + 1) * tq], k_ref, v_ref, o_ref.at[g * tq:(g + 1) * tq],
                   m_ref, l_ref, acc_ref, pl.program_id(2) * (q_ref.shape[0] // tq) + g)


def _mla_sweep(q_ref, k_ref, v_ref, o_ref, m_ref, l_ref, acc_ref, qi):
    tq = tk = q_ref.shape[0]
    heads = q_ref.shape[1] // MLA_QK_PAD
    reps = tk // LANES

    def qk_cols(hd):
        return slice(hd * MLA_QK_PAD, (hd + 1) * MLA_QK_PAD)

    def v_cols(hd):
        return slice(hd * HEAD_DIM, (hd + 1) * HEAD_DIM)

    def scores(rows, keys, hd):
        return lax.dot_general(q_ref[rows, qk_cols(hd)], k_ref[keys, qk_cols(hd)],
                               (((1,), (1,)), ((), ())), preferred_element_type=F32)

    def diagonal_step():
        half = tq // 2
        start = pl.multiple_of(qi * tk, tk)
        tiles = [(hd, r0, r1, nk) for hd in range(heads)
                 for r0, r1, nk in ((0, half, half), (half, tq, tk))]
        s_all = [scores(slice(r0, r1), pl.ds(start, nk), hd) for hd, r0, r1, nk in tiles]
        probs = []
        for (hd, r0, r1, nk), s in zip(tiles, s_all):
            row = lax.broadcasted_iota(jnp.int32, (r1 - r0, nk), 0) + r0
            col = lax.broadcasted_iota(jnp.int32, (r1 - r0, nk), 1)
            s = jnp.where(col <= row, s, -jnp.inf)
            m_new = jnp.broadcast_to(jnp.max(s, axis=-1, keepdims=True), (r1 - r0, LANES))
            p = jnp.exp2(s - jnp.tile(m_new, (1, nk // LANES)))
            m_ref[hd, r0:r1, :] = m_new
            l_ref[hd, r0:r1, :] = jnp.broadcast_to(jnp.sum(p, axis=-1, keepdims=True),
                                                   (r1 - r0, LANES))
            probs.append(p.astype(BF16))
        for (hd, r0, r1, nk), p in zip(tiles, probs):
            acc_ref[hd, r0:r1, :] = jnp.dot(p, v_ref[pl.ds(start, nk), v_cols(hd)],
                                            preferred_element_type=F32)

    def step(c):
        kv_rows = pl.ds(pl.multiple_of(c * tk, tk), tk)
        s_all = [scores(slice(None), kv_rows, hd) for hd in range(heads)]
        probs, corrs = [], []
        for hd, s in enumerate(s_all):
            m_prev = m_ref[hd]
            m_new = jnp.maximum(m_prev, jnp.max(s, axis=-1, keepdims=True))
            corrs.append(jnp.exp2(m_prev - m_new))
            p = jnp.exp2(s - jnp.tile(m_new, (1, reps)))
            l_ref[hd] = l_ref[hd] * corrs[hd] + jnp.sum(p, axis=-1, keepdims=True)
            m_ref[hd] = m_new
            probs.append(p.astype(BF16))
        for hd, (p, corr) in enumerate(zip(probs, corrs)):
            pv = jnp.dot(p, v_ref[kv_rows, v_cols(hd)], preferred_element_type=F32)
            acc_ref[hd] = acc_ref[hd] * corr + pv

    diagonal_step()

    def body(c, carry):
        step(c)
        return carry

    lax.fori_loop(0, qi, body, 0)
    for hd in range(heads):
        o_ref[:, hd * HEAD_DIM:(hd + 1) * HEAD_DIM] = acc_ref[hd] / l_ref[hd]


def _mla_attention(q, k, v, *, batch, seq, t=512, heads_per_step=4, q_blocks_per_step=4):
    M = q.shape[0]
    t = min(t, seq)
    q_blocks_per_step = min(q_blocks_per_step, seq // t)
    tq = t * q_blocks_per_step
    nq = seq // tq
    k3 = k.reshape(batch, seq, k.shape[1])
    v3 = v.reshape(batch, seq, v.shape[1])
    qk_width = heads_per_step * MLA_QK_PAD
    v_width = heads_per_step * HEAD_DIM
    return pl.pallas_call(
        _mla_attn_kernel,
        scratch_shapes=[pltpu.VMEM((heads_per_step, t, LANES), F32),
                        pltpu.VMEM((heads_per_step, t, LANES), F32),
                        pltpu.VMEM((heads_per_step, t, HEAD_DIM), F32)],
        grid=(batch, MLA_HEADS // heads_per_step, nq),
        in_specs=[
            pl.BlockSpec((tq, qk_width), lambda b, h, i: (b * nq + i, h)),
            pl.BlockSpec((None, seq, qk_width), lambda b, h, i: (b, 0, h)),
            pl.BlockSpec((None, seq, v_width), lambda b, h, i: (b, 0, h)),
        ],
        out_specs=pl.BlockSpec((tq, v_width), lambda b, h, i: (b * nq + i, h)),
        out_shape=jax.ShapeDtypeStruct((M, MLA_HEADS * HEAD_DIM), F32),
        compiler_params=_params("arbitrary", "arbitrary", "arbitrary"),
        name="mla_attn",
    )(q, k3, v3)


def _sb_attn_kernel(q_ref, k_ref, v_ref, o_ref, run_ref, *, tq, tk, rq):
    for g in range(q_ref.shape[0] // tq):
        _sb_sweep(q_ref.at[g * tq:(g + 1) * tq], k_ref, v_ref, o_ref.at[g * tq:(g + 1) * tq],
                  run_ref, pl.program_id(2) * (q_ref.shape[0] // tq) + g, tk=tk, rq=rq)


def _sb_sweep(q_ref, k_ref, v_ref, o_ref, run_ref, qi, *, tk, rq):
    tq = q_ref.shape[0]
    heads = q_ref.shape[1] // HEAD_DIM
    blocks_per_step = tq // tk
    chunks = [(hd, r) for hd in range(heads) for r in range(tq // rq)]
    jj = lax.broadcasted_iota(jnp.int32, (tk, tk), 0)
    ss = lax.broadcasted_iota(jnp.int32, (tk, tk), 1)
    suffix = (jj > ss).astype(BF16)
    sign_bit = jnp.int32(-2 ** 31)

    def step(first, masked):
        row = lax.broadcasted_iota(jnp.int32, (rq, tk), 0)
        col = lax.broadcasted_iota(jnp.int32, (rq, tk), 1)
        tiles = []
        for d in reversed(range(blocks_per_step)):
            for u, (_, r) in enumerate(chunks):
                if masked and d * tk >= (r + 1) * rq - 1:
                    continue
                partial = masked and (d + 1) * tk - 1 >= r * rq
                tiles.append((u, d, (col + d * tk < row + r * rq) if partial else None))

        def kv_rows(d):
            return pl.ds(pl.multiple_of((first + d) * tk, tk), tk)

        def cols(u):
            hd = chunks[u][0]
            return slice(hd * HEAD_DIM, (hd + 1) * HEAD_DIM)

        def rows(u):
            r = chunks[u][1]
            return slice(r * rq, (r + 1) * rq)

        runs = [jnp.zeros((rq, LANES), F32) if masked else None for _ in chunks]
        started = [not masked for _ in chunks]
        n = len(tiles)
        zs, log_betas, tails = [None] * n, [None] * n, [None] * n

        def scores(i):
            u, d, _ = tiles[i]
            zs[i] = lax.dot_general(q_ref[rows(u), cols(u)], k_ref[kv_rows(d), cols(u)],
                                    (((1,), (1,)), ((), ())), preferred_element_type=F32)

        def log_terms(i):
            u, d, mask = tiles[i]
            z = zs[i]
            neg_abs = pltpu.bitcast(pltpu.bitcast(z, jnp.int32) | sign_bit, F32)
            softplus = jnp.log(1.0 + jnp.exp2(neg_abs)) * LOG2_E
            log_beta = jnp.minimum(z, 0.0) - softplus
            log_om = log_beta - z
            if mask is not None:
                log_om = jnp.where(mask, log_om, 0.0)
            if runs[u] is None:
                runs[u] = run_ref[u]
            log_betas[i] = log_beta + jnp.tile(runs[u], (1, tk // LANES))
            runs[u] = runs[u] + jnp.sum(log_om, axis=-1, keepdims=True)
            tails[i] = jnp.dot(log_om.astype(BF16), suffix, preferred_element_type=F32)

        def accumulate(i):
            u, d, mask = tiles[i]
            a = jnp.exp2(log_betas[i] + tails[i])
            if mask is not None:
                a = jnp.where(mask, a, 0.0)
            av = jnp.dot(a.astype(BF16), v_ref[kv_rows(d), cols(u)], preferred_element_type=F32)
            o_ref[rows(u), cols(u)] = o_ref[rows(u), cols(u)] + av if started[u] else av
            started[u] = True

        group, lag = 2, 1
        groups = [range(s, min(s + group, n)) for s in range(0, n, group)]
        for t in range(len(groups) + 1 + lag):
            for stage, back in ((scores, 0), (log_terms, 1), (accumulate, 1 + lag)):
                if 0 <= t - back < len(groups):
                    for i in groups[t - back]:
                        stage(i)
        for u in range(len(chunks)):
            if runs[u] is not None:
                run_ref[u] = runs[u]
            if not started[u]:
                o_ref[rows(u), cols(u)] = jnp.zeros((rq, HEAD_DIM), F32)
        return jnp.max(functools.reduce(
            jnp.maximum, [run_ref[u] if run is None else run for u, run in enumerate(runs)]))

    top = step(qi * blocks_per_step, True)

    def body(state):
        n, _ = state
        return n + 1, step((qi - 1 - n) * blocks_per_step, False)

    lax.while_loop(lambda state: (state[0] < qi) & (state[1] > SB_DEAD_LOG2), body,
                   (jnp.int32(0), top))


def _sb_attention(q, k, v, *, batch, seq, tq=256, tk=256, rq=256, heads_per_step=8,
                  q_blocks_per_step=4):
    M = q.shape[0]
    t = min(tq, seq)
    tk = min(tk, t)
    rq = min(rq, t)
    q_blocks_per_step = min(q_blocks_per_step, seq // t)
    rows = t * q_blocks_per_step
    nq = seq // rows
    width = heads_per_step * HEAD_DIM
    k3 = k.reshape(batch, seq, k.shape[1])
    v3 = v.reshape(batch, seq, v.shape[1])
    return pl.pallas_call(
        functools.partial(_sb_attn_kernel, tq=t, tk=tk, rq=rq),
        scratch_shapes=[pltpu.VMEM((heads_per_step * (t // rq), rq, LANES), F32)],
        grid=(batch, SB_HEADS // heads_per_step, nq),
        in_specs=[
            pl.BlockSpec((rows, width), lambda b, h, i: (b * nq + i, h)),
            pl.BlockSpec((None, seq, width), lambda b, h, i: (b, 0, h)),
            pl.BlockSpec((None, seq, width), lambda b, h, i: (b, 0, h)),
        ],
        out_specs=pl.BlockSpec((rows, width), lambda b, h, i: (b * nq + i, h)),
        out_shape=jax.ShapeDtypeStruct((M, SB_HEADS * HEAD_DIM), F32),
        compiler_params=_params("arbitrary", "arbitrary", "arbitrary"),
        name="sb_attn",
    )(q, k3, v3)


def _outproj_kernel(x_ref, om_ref, os_ref, gt_ref, mg_ref, sg_ref, wom_ref, wos_ref,
                    lng_ref, lnb_ref, o_ref, *, alpha):
    rows_total = x_ref.shape[0]
    chunk = min(OUTPROJ_ROW_CHUNK, rows_total)
    gate = (1.0 + gt_ref[...]) / alpha
    for r0 in range(0, rows_total, chunk):
        rows = slice(r0, r0 + chunk)
        om = _rms_norm(om_ref[rows, :], mg_ref[...]).astype(BF16)
        os_ = _rms_norm(os_ref[rows, :], sg_ref[...]).astype(BF16)
        m = (jnp.dot(om, wom_ref[...], preferred_element_type=F32)
             + jnp.dot(os_, wos_ref[...], preferred_element_type=F32))
        y = x_ref[rows, :] + gate * m
        o_ref[rows, :] = _layer_norm(y, lng_ref[...], lnb_ref[...], LN_EPS / (alpha * alpha))


def _outproj_block(x, o_mla, o_sb, ada, l, k_gate, w, ln_g, ln_b, k_ln, *, alpha,
                   rows_per_batch, tm=512):
    M, D = x.shape
    tm = min(tm, rows_per_batch)
    tiles_per_batch = rows_per_batch // tm

    def rows(width):
        return pl.BlockSpec((tm, width), lambda i: (i, 0))

    consts = [w["mla_out_g"], w["sb_out_g"], w["o_mla"], w["o_sb"]]
    ln_spec = pl.BlockSpec((None, None, 1, D), lambda i: (l, k_ln, 0, 0))
    return pl.pallas_call(
        functools.partial(_outproj_kernel, alpha=alpha),
        grid=(M // tm,),
        in_specs=[rows(D), rows(o_mla.shape[1]), rows(o_sb.shape[1]),
                  pl.BlockSpec((None, None, None, 1, D),
                               lambda i: (l, i // tiles_per_batch, k_gate, 0, 0))]
                 + [_layer_resident(a, l) for a in consts] + [ln_spec, ln_spec],
        out_specs=rows(D),
        out_shape=jax.ShapeDtypeStruct((M, D), F32),
        compiler_params=_params("arbitrary"),
        name="outproj",
    )(x, o_mla, o_sb, ada, *consts, ln_g, ln_b)


def _mixer_weights(w_in, q_norm_g, kv_norm_g, w_uq, w_ukv, mla_out_g, sb_out_g, w_o):
    depth = w_in.shape[0]
    o1 = Q_LORA_RANK
    o2 = o1 + KV_LORA_RANK
    o3 = o2 + QK_ROPE_DIM
    lane_pad = ((0, 0),) * 2 + ((0, LANES - QK_ROPE_DIM),)
    uq = w_uq.astype(BF16).reshape(depth, Q_LORA_RANK, MLA_HEADS, MLA_QK_DIM)
    uq_rope = jnp.pad(uq[..., QK_NOPE_DIM:], ((0, 0),) + lane_pad)
    ukv = w_ukv.astype(BF16).reshape(depth, KV_LORA_RANK, MLA_HEADS, QK_NOPE_DIM + HEAD_DIM)
    mla_w = MLA_HEADS * HEAD_DIM
    w_in = w_in.astype(BF16)
    return {
        "cq": w_in[:, :, :o1],
        "ckv": w_in[:, :, o1:o2],
        "kr": jnp.pad(w_in[:, :, o2:o3], lane_pad),
        "sb": w_in[:, :, o3:],
        "qn_g": q_norm_g.reshape(depth, 1, -1),
        "kvn_g": kv_norm_g.reshape(depth, 1, -1),
        "uq_nope": uq[..., :QK_NOPE_DIM].reshape(depth, Q_LORA_RANK, MLA_HEADS * QK_NOPE_DIM),
        "uq_rope": uq_rope.reshape(depth, Q_LORA_RANK, MLA_HEADS * LANES),
        "uk": ukv[..., :QK_NOPE_DIM].reshape(depth, KV_LORA_RANK, MLA_HEADS * QK_NOPE_DIM),
        "uv": ukv[..., QK_NOPE_DIM:].reshape(depth, KV_LORA_RANK, MLA_HEADS * HEAD_DIM),
        "mla_out_g": mla_out_g.reshape(depth, 1, -1),
        "sb_out_g": sb_out_g.reshape(depth, 1, -1),
        "o_mla": w_o[:, :mla_w].astype(BF16),
        "o_sb": w_o[:, mla_w:].astype(BF16),
    }


def kernel(x, c, positions, ada_w, ada_b, ln_g, ln_b, ffn1_wi, ffn1_wo, w_in, q_norm_g, kv_norm_g, w_uq, w_ukv, mla_out_g, sb_out_g, w_o, ffn2_wi, ffn2_wo):
    B, S, D = x.shape
    depth = ada_w.shape[0]
    alpha = (2.0 * depth) ** 0.25
    M = B * S

    ada = _ada_all_layers(c, ada_w, ada_b).reshape(depth, B, N_ADA, 1, D)
    tables = _rope_tables(positions)
    ln_g = ln_g.reshape(depth, 3, 1, D)
    ln_b = ln_b.reshape(depth, 3, 1, D)
    ffn1 = (ffn1_wi.astype(BF16), ffn1_wo.astype(BF16))
    ffn2 = (ffn2_wi.astype(BF16), ffn2_wo.astype(BF16))
    w = _mixer_weights(w_in, q_norm_g, kv_norm_g, w_uq, w_ukv, mla_out_g, sb_out_g, w_o)
    xf = x.reshape(M, D)
    for l in range(depth):
        xf = _ffn_block(xf, ada, l, 0, *ffn1, ln_g, ln_b, 0, alpha=alpha, rows_per_batch=S)
        q, k, v, sq, sk, sv = _inproj_block(xf, ada, l, 3, tables, w, rows_per_batch=S)
        o_mla = _mla_attention(q, k, v, batch=B, seq=S)
        o_sb = _sb_attention(sq, sk, sv, batch=B, seq=S)
        xf = _outproj_block(xf, o_mla, o_sb, ada, l, 5, w, ln_g, ln_b, 1,
                            alpha=alpha, rows_per_batch=S)
        xf = _ffn_block(xf, ada, l, 6, *ffn2, ln_g, ln_b, 2, alpha=alpha, rows_per_batch=S)
    return xf.reshape(B, S, D)
```

```python
import functools

import jax
import jax.numpy as jnp
from jax import lax
from jax.experimental import pallas as pl
from jax.experimental.pallas import tpu as pltpu

HEAD_DIM = 128
MLA_HEADS = 8
SB_HEADS = 8
Q_LORA_RANK = 768
KV_LORA_RANK = 512
QK_NOPE_DIM = 128
QK_ROPE_DIM = 64
MLA_QK_DIM = QK_NOPE_DIM + QK_ROPE_DIM
ROPE_THETA = 10000.0
N_ADA = 9
LN_EPS = 1e-5
RMS_EPS = 1e-6
FFN_RESIDUAL_WEIGHT = 0.5
LOG2_E = 1.4426950408889634
SB_DEAD_LOG2 = -152.0

LANES = 128
MLA_QK_PAD = 2 * LANES
VMEM_LIMIT_BYTES = 60 * 1024 * 1024
OUTPROJ_ROW_CHUNK = 256
FFN_EPILOGUE_ROW_CHUNK = 256

F32 = jnp.float32
BF16 = jnp.bfloat16


def _params(*semantics):
    return pltpu.CompilerParams(dimension_semantics=semantics,
                                vmem_limit_bytes=VMEM_LIMIT_BYTES)


def _layer_resident(stacked, l):
    index = (l,) + (0,) * (stacked.ndim - 1)
    return pl.BlockSpec((None,) + stacked.shape[1:], lambda *_: index,
                        pipeline_mode=pl.Buffered(1))


def _layer_norm(y, g, b, eps=LN_EPS):
    mu = jnp.mean(y, axis=-1, keepdims=True)
    d = y - mu
    var = jnp.mean(d * d, axis=-1, keepdims=True)
    return d * lax.rsqrt(var + eps) * g + b


def _rms_norm(y, g):
    return y * lax.rsqrt(jnp.mean(y * y, axis=-1, keepdims=True) + RMS_EPS) * g


def _ada_kernel(c_ref, w_ref, b_ref, o_ref):
    c = c_ref[...]
    c_act = (c * jax.nn.sigmoid(c)).astype(BF16)
    acc = jnp.dot(c_act, w_ref[...].astype(BF16), preferred_element_type=F32)
    o_ref[...] = acc + b_ref[...]


def _ada_all_layers(c, ada_w, ada_b, tn=2048):
    L, D, N = ada_w.shape
    B = c.shape[0]
    tn = min(tn, D)
    assert D % tn == 0
    return pl.pallas_call(
        _ada_kernel,
        grid=(L, N // tn),
        in_specs=[
            pl.BlockSpec((B, D), lambda l, j: (0, 0)),
            pl.BlockSpec((None, D, tn), lambda l, j: (l, 0, j)),
            pl.BlockSpec((None, 1, tn), lambda l, j: (l, 0, j)),
        ],
        out_specs=pl.BlockSpec((None, B, tn), lambda l, j: (l, 0, j)),
        out_shape=jax.ShapeDtypeStruct((L, B, N), F32),
        compiler_params=_params("arbitrary", "arbitrary"),
        name="ada",
    )(c, ada_w, ada_b.reshape(L, 1, N))


def _rope_table_kernel(pos_ref, invf_ref, cos_ref, sinlo_ref, sinhi_ref):
    half = QK_ROPE_DIM // 2
    ang = pos_ref[...].astype(F32) * invf_ref[...]
    lane = lax.broadcasted_iota(jnp.int32, ang.shape, 1)
    cos = jnp.cos(ang)
    sin = jnp.sin(ang)
    cos_ref[...] = jnp.where(lane < QK_ROPE_DIM, cos, 0.0)
    sinlo_ref[...] = jnp.where(lane < half, -sin, 0.0)
    sinhi_ref[...] = jnp.where((lane >= half) & (lane < QK_ROPE_DIM), sin, 0.0)


def _rope_tables(positions, tm=1024):
    M = positions.size
    half = QK_ROPE_DIM // 2
    inv_freq = ROPE_THETA ** (-jnp.arange(half, dtype=F32) / half)
    invf = jnp.tile(inv_freq, LANES // half).reshape(1, LANES)
    tm = min(tm, M)
    spec = pl.BlockSpec((tm, LANES), lambda i: (i, 0))
    shape = jax.ShapeDtypeStruct((M, LANES), F32)
    return pl.pallas_call(
        _rope_table_kernel,
        grid=(M // tm,),
        in_specs=[pl.BlockSpec((tm, 1), lambda i: (i, 0)),
                  pl.BlockSpec((1, LANES), lambda i: (0, 0))],
        out_specs=[spec, spec, spec],
        out_shape=[shape, shape, shape],
        compiler_params=_params("arbitrary"),
        name="rope_tables",
    )(positions.reshape(M, 1), invf)


def _ffn_kernel(x_ref, sh_ref, sc_ref, gt_ref, wg_ref, wu_ref, wo_ref, lng_ref, lnb_ref,
                o_ref, *, alpha, hidden_tiles):
    j = pl.program_id(1)
    last = hidden_tiles - 1

    def gated():
        h = (x_ref[...] * (1.0 + sc_ref[...]) + sh_ref[...]).astype(BF16)
        g = jnp.dot(h, wg_ref[...], preferred_element_type=F32)
        u = jnp.dot(h, wu_ref[...], preferred_element_type=F32)
        return (g * jax.nn.sigmoid(g) * u).astype(BF16)

    if hidden_tiles > 1:
        @pl.when(j == 0)
        def _():
            o_ref[...] = jnp.dot(gated(), wo_ref[...], preferred_element_type=F32)

        @pl.when((j > 0) & (j < last))
        def _():
            o_ref[...] += jnp.dot(gated(), wo_ref[...], preferred_element_type=F32)

    @pl.when(j == last)
    def _():
        a = gated()
        c = (FFN_RESIDUAL_WEIGHT / alpha) * (1.0 + gt_ref[...])
        chunk = min(FFN_EPILOGUE_ROW_CHUNK, a.shape[0])
        for r0 in range(0, a.shape[0], chunk):
            rows = slice(r0, r0 + chunk)
            f = jnp.dot(a[rows], wo_ref[...], preferred_element_type=F32)
            if hidden_tiles > 1:
                f = o_ref[rows, :] + f
            y = x_ref[rows, :] + c * f
            o_ref[rows, :] = _layer_norm(y, lng_ref[...], lnb_ref[...], LN_EPS / (alpha * alpha))


def _ffn_block(x, ada, l, k_shift, wi, wo, ln_g, ln_b, k_ln, *, alpha, rows_per_batch,
               tm=1024, tf=512):
    M, D = x.shape
    F = wo.shape[1]
    tm = min(tm, rows_per_batch)
    tf = min(tf, F)
    nf = F // tf
    tiles_per_batch = rows_per_batch // tm

    def ada_spec(k):
        return pl.BlockSpec((None, None, None, 1, D),
                            lambda i, j: (l, i // tiles_per_batch, k, 0, 0))

    ln_spec = pl.BlockSpec((None, None, 1, D), lambda i, j: (l, k_ln, 0, 0))
    return pl.pallas_call(
        functools.partial(_ffn_kernel, alpha=alpha, hidden_tiles=nf),
        grid=(M // tm, nf),
        in_specs=[
            pl.BlockSpec((tm, D), lambda i, j: (i, 0)),
            ada_spec(k_shift), ada_spec(k_shift + 1), ada_spec(k_shift + 2),
            pl.BlockSpec((None, D, tf), lambda i, j: (l, 0, j)),
            pl.BlockSpec((None, D, tf), lambda i, j: (l, 0, j + nf)),
            pl.BlockSpec((None, tf, D), lambda i, j: (l, j, 0)),
            ln_spec, ln_spec,
        ],
        out_specs=pl.BlockSpec((tm, D), lambda i, j: (i, 0)),
        out_shape=jax.ShapeDtypeStruct((M, D), F32),
        compiler_params=_params("arbitrary", "arbitrary"),
        name="ffn",
    )(x, ada, ada, ada, wi, wi, wo, ln_g, ln_b)


def _inproj_kernel(x_ref, sh_ref, sc_ref, cos_ref, sinlo_ref, sinhi_ref,
                   wcq_ref, wckv_ref, wkr_ref, wsb_ref, qng_ref, kvng_ref,
                   wuqn_ref, wuqr_ref, wuk_ref, wuv_ref,
                   q_ref, k_ref, v_ref, sq_ref, sk_ref, sv_ref):
    h = (x_ref[...] * (1.0 + sc_ref[...]) + sh_ref[...]).astype(BF16)
    cos, sin_lo, sin_hi = cos_ref[...], sinlo_ref[...], sinhi_ref[...]
    half = QK_ROPE_DIM // 2

    def rope(xr):
        return (xr * cos + pltpu.roll(xr, LANES - half, axis=1) * sin_lo
                + pltpu.roll(xr, half, axis=1) * sin_hi)

    c_q = jnp.dot(h, wcq_ref[...], preferred_element_type=F32)
    c_kv = jnp.dot(h, wckv_ref[...], preferred_element_type=F32)
    k_rope = jnp.dot(h, wkr_ref[...], preferred_element_type=F32)

    sb_w = SB_HEADS * HEAD_DIM
    sb = jnp.dot(h, wsb_ref[...], preferred_element_type=F32)
    sq_ref[...] = (sb[:, :sb_w] * (LOG2_E * HEAD_DIM ** -0.5)).astype(BF16)
    sk_ref[...] = sb[:, sb_w:2 * sb_w].astype(BF16)
    sv_ref[...] = sb[:, 2 * sb_w:].astype(BF16)

    c_qn = _rms_norm(c_q, qng_ref[...]).astype(BF16)
    c_kvn = _rms_norm(c_kv, kvng_ref[...]).astype(BF16)
    q_nope = jnp.dot(c_qn, wuqn_ref[...], preferred_element_type=F32)
    q_rope = jnp.dot(c_qn, wuqr_ref[...], preferred_element_type=F32)
    k_nope = jnp.dot(c_kvn, wuk_ref[...], preferred_element_type=F32)
    v_ref[...] = jnp.dot(c_kvn, wuv_ref[...], preferred_element_type=F32).astype(BF16)
    q_scale = LOG2_E * MLA_QK_DIM ** -0.5
    for hd in range(MLA_HEADS):
        lo = hd * MLA_QK_PAD
        q_ref[:, lo:lo + LANES] = (q_nope[:, hd * LANES:(hd + 1) * LANES] * q_scale).astype(BF16)
        q_ref[:, lo + LANES:lo + 2 * LANES] = (
            rope(q_rope[:, hd * LANES:(hd + 1) * LANES]) * q_scale).astype(BF16)

    k_pe = rope(k_rope).astype(BF16)
    for hd in range(MLA_HEADS):
        lo = hd * MLA_QK_PAD
        k_ref[:, lo:lo + LANES] = k_nope[:, hd * LANES:(hd + 1) * LANES].astype(BF16)
        k_ref[:, lo + LANES:lo + 2 * LANES] = k_pe


def _inproj_block(x, ada, l, k_shift, tables, w, *, rows_per_batch, tm=512):
    M, D = x.shape
    tm = min(tm, rows_per_batch)
    tiles_per_batch = rows_per_batch // tm

    def ada_spec(k):
        return pl.BlockSpec((None, None, None, 1, D),
                            lambda i: (l, i // tiles_per_batch, k, 0, 0))

    def rows(width):
        return pl.BlockSpec((tm, width), lambda i: (i, 0))

    weights = [w["cq"], w["ckv"], w["kr"], w["sb"], w["qn_g"], w["kvn_g"],
               w["uq_nope"], w["uq_rope"], w["uk"], w["uv"]]
    widths = [MLA_HEADS * MLA_QK_PAD, MLA_HEADS * MLA_QK_PAD, MLA_HEADS * HEAD_DIM,
              SB_HEADS * HEAD_DIM, SB_HEADS * HEAD_DIM, SB_HEADS * HEAD_DIM]
    return pl.pallas_call(
        _inproj_kernel,
        grid=(M // tm,),
        in_specs=[rows(D), ada_spec(k_shift), ada_spec(k_shift + 1),
                  rows(LANES), rows(LANES), rows(LANES)]
                 + [_layer_resident(a, l) for a in weights],
        out_specs=[rows(n) for n in widths],
        out_shape=[jax.ShapeDtypeStruct((M, n), BF16) for n in widths],
        compiler_params=_params("arbitrary"),
        name="inproj",
    )(x, ada, ada, *tables, *weights)


def _mla_attn_kernel(q_ref, k_ref, v_ref, o_ref, m_ref, l_ref, acc_ref):
    tq = tk = m_ref.shape[1]
    for g in range(q_ref.shape[0] // tq):
        _mla_sweep(q_ref.at[g * tq:(g + 1) * tq], k_ref, v_ref, o_ref.at[g * tq:(g + 1) * tq],
                   m_ref, l_ref, acc_ref, pl.program_id(2) * (q_ref.shape[0] // tq) + g)


def _mla_sweep(q_ref, k_ref, v_ref, o_ref, m_ref, l_ref, acc_ref, qi):
    tq = tk = q_ref.shape[0]
    heads = q_ref.shape[1] // MLA_QK_PAD
    reps = tk // LANES

    def qk_cols(hd):
        return slice(hd * MLA_QK_PAD, (hd + 1) * MLA_QK_PAD)

    def v_cols(hd):
        return slice(hd * HEAD_DIM, (hd + 1) * HEAD_DIM)

    def scores(rows, keys, hd):
        return lax.dot_general(q_ref[rows, qk_cols(hd)], k_ref[keys, qk_cols(hd)],
                               (((1,), (1,)), ((), ())), preferred_element_type=F32)

    def diagonal_step():
        half = tq // 2
        start = pl.multiple_of(qi * tk, tk)
        tiles = [(hd, r0, r1, nk) for hd in range(heads)
                 for r0, r1, nk in ((0, half, half), (half, tq, tk))]
        s_all = [scores(slice(r0, r1), pl.ds(start, nk), hd) for hd, r0, r1, nk in tiles]
        probs = []
        for (hd, r0, r1, nk), s in zip(tiles, s_all):
            row = lax.broadcasted_iota(jnp.int32, (r1 - r0, nk), 0) + r0
            col = lax.broadcasted_iota(jnp.int32, (r1 - r0, nk), 1)
            s = jnp.where(col <= row, s, -jnp.inf)
            m_new = jnp.broadcast_to(jnp.max(s, axis=-1, keepdims=True), (r1 - r0, LANES))
            p = jnp.exp2(s - jnp.tile(m_new, (1, nk // LANES)))
            m_ref[hd, r0:r1, :] = m_new
            l_ref[hd, r0:r1, :] = jnp.broadcast_to(jnp.sum(p, axis=-1, keepdims=True),
                                                   (r1 - r0, LANES))
            probs.append(p.astype(BF16))
        for (hd, r0, r1, nk), p in zip(tiles, probs):
            acc_ref[hd, r0:r1, :] = jnp.dot(p, v_ref[pl.ds(start, nk), v_cols(hd)],
                                            preferred_element_type=F32)

    def step(c):
        kv_rows = pl.ds(pl.multiple_of(c * tk, tk), tk)
        s_all = [scores(slice(None), kv_rows, hd) for hd in range(heads)]
        probs, corrs = [], []
        for hd, s in enumerate(s_all):
            m_prev = m_ref[hd]
            m_new = jnp.maximum(m_prev, jnp.max(s, axis=-1, keepdims=True))
            corrs.append(jnp.exp2(m_prev - m_new))
            p = jnp.exp2(s - jnp.tile(m_new, (1, reps)))
            l_ref[hd] = l_ref[hd] * corrs[hd] + jnp.sum(p, axis=-1, keepdims=True)
            m_ref[hd] = m_new
            probs.append(p.astype(BF16))
        for hd, (p, corr) in enumerate(zip(probs, corrs)):
            pv = jnp.dot(p, v_ref[kv_rows, v_cols(hd)], preferred_element_type=F32)
            acc_ref[hd] = acc_ref[hd] * corr + pv

    diagonal_step()

    def body(c, carry):
        step(c)
        return carry

    lax.fori_loop(0, qi, body, 0)
    for hd in range(heads):
        o_ref[:, hd * HEAD_DIM:(hd + 1) * HEAD_DIM] = acc_ref[hd] / l_ref[hd]


def _mla_attention(q, k, v, *, batch, seq, t=512, heads_per_step=4, q_blocks_per_step=4):
    M = q.shape[0]
    t = min(t, seq)
    q_blocks_per_step = min(q_blocks_per_step, seq // t)
    tq = t * q_blocks_per_step
    nq = seq // tq
    k3 = k.reshape(batch, seq, k.shape[1])
    v3 = v.reshape(batch, seq, v.shape[1])
    qk_width = heads_per_step * MLA_QK_PAD
    v_width = heads_per_step * HEAD_DIM
    return pl.pallas_call(
        _mla_attn_kernel,
        scratch_shapes=[pltpu.VMEM((heads_per_step, t, LANES), F32),
                        pltpu.VMEM((heads_per_step, t, LANES), F32),
                        pltpu.VMEM((heads_per_step, t, HEAD_DIM), F32)],
        grid=(batch, MLA_HEADS // heads_per_step, nq),
        in_specs=[
            pl.BlockSpec((tq, qk_width), lambda b, h, i: (b * nq + i, h)),
            pl.BlockSpec((None, seq, qk_width), lambda b, h, i: (b, 0, h)),
            pl.BlockSpec((None, seq, v_width), lambda b, h, i: (b, 0, h)),
        ],
        out_specs=pl.BlockSpec((tq, v_width), lambda b, h, i: (b * nq + i, h)),
        out_shape=jax.ShapeDtypeStruct((M, MLA_HEADS * HEAD_DIM), F32),
        compiler_params=_params("arbitrary", "arbitrary", "arbitrary"),
        name="mla_attn",
    )(q, k3, v3)


def _sb_attn_kernel(q_ref, k_ref, v_ref, o_ref, run_ref, *, tq, tk, rq):
    for g in range(q_ref.shape[0] // tq):
        _sb_sweep(q_ref.at[g * tq:(g + 1) * tq], k_ref, v_ref, o_ref.at[g * tq:(g + 1) * tq],
                  run_ref, pl.program_id(2) * (q_ref.shape[0] // tq) + g, tk=tk, rq=rq)


def _sb_sweep(q_ref, k_ref, v_ref, o_ref, run_ref, qi, *, tk, rq):
    tq = q_ref.shape[0]
    heads = q_ref.shape[1] // HEAD_DIM
    blocks_per_step = tq // tk
    chunks = [(hd, r) for hd in range(heads) for r in range(tq // rq)]
    jj = lax.broadcasted_iota(jnp.int32, (tk, tk), 0)
    ss = lax.broadcasted_iota(jnp.int32, (tk, tk), 1)
    suffix = (jj > ss).astype(BF16)
    sign_bit = jnp.int32(-2 ** 31)

    def step(first, masked):
        row = lax.broadcasted_iota(jnp.int32, (rq, tk), 0)
        col = lax.broadcasted_iota(jnp.int32, (rq, tk), 1)
        tiles = []
        for d in reversed(range(blocks_per_step)):
            for u, (_, r) in enumerate(chunks):
                if masked and d * tk >= (r + 1) * rq - 1:
                    continue
                partial = masked and (d + 1) * tk - 1 >= r * rq
                tiles.append((u, d, (col + d * tk < row + r * rq) if partial else None))

        def kv_rows(d):
            return pl.ds(pl.multiple_of((first + d) * tk, tk), tk)

        def cols(u):
            hd = chunks[u][0]
            return slice(hd * HEAD_DIM, (hd + 1) * HEAD_DIM)

        def rows(u):
            r = chunks[u][1]
            return slice(r * rq, (r + 1) * rq)

        runs = [jnp.zeros((rq, LANES), F32) if masked else None for _ in chunks]
        started = [not masked for _ in chunks]
        n = len(tiles)
        zs, log_betas, tails = [None] * n, [None] * n, [None] * n

        def scores(i):
            u, d, _ = tiles[i]
            zs[i] = lax.dot_general(q_ref[rows(u), cols(u)], k_ref[kv_rows(d), cols(u)],
                                    (((1,), (1,)), ((), ())), preferred_element_type=F32)

        def log_terms(i):
            u, d, mask = tiles[i]
            z = zs[i]
            neg_abs = pltpu.bitcast(pltpu.bitcast(z, jnp.int32) | sign_bit, F32)
            softplus = jnp.log(1.0 + jnp.exp2(neg_abs)) * LOG2_E
            log_beta = jnp.minimum(z, 0.0) - softplus
            log_om = log_beta - z
            if mask is not None:
                log_om = jnp.where(mask, log_om, 0.0)
            if runs[u] is None:
                runs[u] = run_ref[u]
            log_betas[i] = log_beta + jnp.tile(runs[u], (1, tk // LANES))
            runs[u] = runs[u] + jnp.sum(log_om, axis=-1, keepdims=True)
            tails[i] = jnp.dot(log_om.astype(BF16), suffix, preferred_element_type=F32)

        def accumulate(i):
            u, d, mask = tiles[i]
            a = jnp.exp2(log_betas[i] + tails[i])
            if mask is not None:
                a = jnp.where(mask, a, 0.0)
            av = jnp.dot(a.astype(BF16), v_ref[kv_rows(d), cols(u)], preferred_element_type=F32)
            o_ref[rows(u), cols(u)] = o_ref[rows(u), cols(u)] + av if started[u] else av
            started[u] = True

        group, lag = 2, 1
        groups = [range(s, min(s + group, n)) for s in range(0, n, group)]
        for t in range(len(groups) + 1 + lag):
            for stage, back in ((scores, 0), (log_terms, 1), (accumulate, 1 + lag)):
                if 0 <= t - back < len(groups):
                    for i in groups[t - back]:
                        stage(i)
        for u in range(len(chunks)):
            if runs[u] is not None:
                run_ref[u] = runs[u]
            if not started[u]:
                o_ref[rows(u), cols(u)] = jnp.zeros((rq, HEAD_DIM), F32)
        return jnp.max(functools.reduce(
            jnp.maximum, [run_ref[u] if run is None else run for u, run in enumerate(runs)]))

    top = step(qi * blocks_per_step, True)

    def body(state):
        n, _ = state
        return n + 1, step((qi - 1 - n) * blocks_per_step, False)

    lax.while_loop(lambda state: (state[0] < qi) & (state[1] > SB_DEAD_LOG2), body,
                   (jnp.int32(0), top))


def _sb_attention(q, k, v, *, batch, seq, tq=256, tk=256, rq=256, heads_per_step=8,
                  q_blocks_per_step=4):
    M = q.shape[0]
    t = min(tq, seq)
    tk = min(tk, t)
    rq = min(rq, t)
    q_blocks_per_step = min(q_blocks_per_step, seq // t)
    rows = t * q_blocks_per_step
    nq = seq // rows
    width = heads_per_step * HEAD_DIM
    k3 = k.reshape(batch, seq, k.shape[1])
    v3 = v.reshape(batch, seq, v.shape[1])
    return pl.pallas_call(
        functools.partial(_sb_attn_kernel, tq=t, tk=tk, rq=rq),
        scratch_shapes=[pltpu.VMEM((heads_per_step * (t // rq), rq, LANES), F32)],
        grid=(batch, SB_HEADS // heads_per_step, nq),
        in_specs=[
            pl.BlockSpec((rows, width), lambda b, h, i: (b * nq + i, h)),
            pl.BlockSpec((None, seq, width), lambda b, h, i: (b, 0, h)),
            pl.BlockSpec((None, seq, width), lambda b, h, i: (b, 0, h)),
        ],
        out_specs=pl.BlockSpec((rows, width), lambda b, h, i: (b * nq + i, h)),
        out_shape=jax.ShapeDtypeStruct((M, SB_HEADS * HEAD_DIM), F32),
        compiler_params=_params("arbitrary", "arbitrary", "arbitrary"),
        name="sb_attn",
    )(q, k3, v3)


def _outproj_kernel(x_ref, om_ref, os_ref, gt_ref, mg_ref, sg_ref, wom_ref, wos_ref,
                    lng_ref, lnb_ref, o_ref, *, alpha):
    rows_total = x_ref.shape[0]
    chunk = min(OUTPROJ_ROW_CHUNK, rows_total)
    gate = (1.0 + gt_ref[...]) / alpha
    for r0 in range(0, rows_total, chunk):
        rows = slice(r0, r0 + chunk)
        om = _rms_norm(om_ref[rows, :], mg_ref[...]).astype(BF16)
        os_ = _rms_norm(os_ref[rows, :], sg_ref[...]).astype(BF16)
        m = (jnp.dot(om, wom_ref[...], preferred_element_type=F32)
             + jnp.dot(os_, wos_ref[...], preferred_element_type=F32))
        y = x_ref[rows, :] + gate * m
        o_ref[rows, :] = _layer_norm(y, lng_ref[...], lnb_ref[...], LN_EPS / (alpha * alpha))


def _outproj_block(x, o_mla, o_sb, ada, l, k_gate, w, ln_g, ln_b, k_ln, *, alpha,
                   rows_per_batch, tm=512):
    M, D = x.shape
    tm = min(tm, rows_per_batch)
    tiles_per_batch = rows_per_batch // tm

    def rows(width):
        return pl.BlockSpec((tm, width), lambda i: (i, 0))

    half = o_mla.shape[1]
    assert o_sb.shape[1] == half and w["o"].shape[1] == 2 * half

    def w_o_half(k):
        return pl.BlockSpec((None, half, D), lambda i: (l, k, 0), pipeline_mode=pl.Buffered(1))

    gains = [w["mla_out_g"], w["sb_out_g"]]
    ln_spec = pl.BlockSpec((None, None, 1, D), lambda i: (l, k_ln, 0, 0))
    return pl.pallas_call(
        functools.partial(_outproj_kernel, alpha=alpha),
        grid=(M // tm,),
        in_specs=[rows(D), rows(half), rows(half),
                  pl.BlockSpec((None, None, None, 1, D),
                               lambda i: (l, i // tiles_per_batch, k_gate, 0, 0))]
                 + [_layer_resident(a, l) for a in gains]
                 + [w_o_half(0), w_o_half(1), ln_spec, ln_spec],
        out_specs=rows(D),
        out_shape=jax.ShapeDtypeStruct((M, D), F32),
        compiler_params=_params("arbitrary"),
        name="outproj",
    )(x, o_mla, o_sb, ada, *gains, w["o"], w["o"], ln_g, ln_b)


def _mixer_weights(w_in, q_norm_g, kv_norm_g, w_uq, w_ukv, mla_out_g, sb_out_g, w_o):
    depth = w_in.shape[0]
    o1 = Q_LORA_RANK
    o2 = o1 + KV_LORA_RANK
    o3 = o2 + QK_ROPE_DIM
    lane_pad = ((0, 0),) * 2 + ((0, LANES - QK_ROPE_DIM),)
    uq = w_uq.astype(BF16).reshape(depth, Q_LORA_RANK, MLA_HEADS, MLA_QK_DIM)
    uq_rope = jnp.pad(uq[..., QK_NOPE_DIM:], ((0, 0),) + lane_pad)
    ukv = w_ukv.astype(BF16).reshape(depth, KV_LORA_RANK, MLA_HEADS, QK_NOPE_DIM + HEAD_DIM)
    w_in = w_in.astype(BF16)
    return {
        "cq": w_in[:, :, :o1],
        "ckv": w_in[:, :, o1:o2],
        "kr": jnp.pad(w_in[:, :, o2:o3], lane_pad),
        "sb": w_in[:, :, o3:],
        "qn_g": q_norm_g.reshape(depth, 1, -1),
        "kvn_g": kv_norm_g.reshape(depth, 1, -1),
        "uq_nope": uq[..., :QK_NOPE_DIM].reshape(depth, Q_LORA_RANK, MLA_HEADS * QK_NOPE_DIM),
        "uq_rope": uq_rope.reshape(depth, Q_LORA_RANK, MLA_HEADS * LANES),
        "uk": ukv[..., :QK_NOPE_DIM].reshape(depth, KV_LORA_RANK, MLA_HEADS * QK_NOPE_DIM),
        "uv": ukv[..., QK_NOPE_DIM:].reshape(depth, KV_LORA_RANK, MLA_HEADS * HEAD_DIM),
        "mla_out_g": mla_out_g.reshape(depth, 1, -1),
        "sb_out_g": sb_out_g.reshape(depth, 1, -1),
        "o": w_o.astype(BF16),
    }


def kernel(x, c, positions, ada_w, ada_b, ln_g, ln_b, ffn1_wi, ffn1_wo, w_in, q_norm_g, kv_norm_g, w_uq, w_ukv, mla_out_g, sb_out_g, w_o, ffn2_wi, ffn2_wo):
    B, S, D = x.shape
    depth = ada_w.shape[0]
    alpha = (2.0 * depth) ** 0.25
    M = B * S

    ada = _ada_all_layers(c, ada_w, ada_b).reshape(depth, B, N_ADA, 1, D)
    tables = _rope_tables(positions)
    ln_g = ln_g.reshape(depth, 3, 1, D)
    ln_b = ln_b.reshape(depth, 3, 1, D)
    ffn1 = (ffn1_wi.astype(BF16), ffn1_wo.astype(BF16))
    ffn2 = (ffn2_wi.astype(BF16), ffn2_wo.astype(BF16))
    w = _mixer_weights(w_in, q_norm_g, kv_norm_g, w_uq, w_ukv, mla_out_g, sb_out_g, w_o)
    xf = x.reshape(M, D)
    for l in range(depth):
        xf = _ffn_block(xf, ada, l, 0, *ffn1, ln_g, ln_b, 0, alpha=alpha, rows_per_batch=S)
        q, k, v, sq, sk, sv = _inproj_block(xf, ada, l, 3, tables, w, rows_per_batch=S)
        o_mla = _mla_attention(q, k, v, batch=B, seq=S)
        o_sb = _sb_attention(sq, sk, sv, batch=B, seq=S)
        xf = _outproj_block(xf, o_mla, o_sb, ada, l, 5, w, ln_g, ln_b, 1,
                            alpha=alpha, rows_per_batch=S)
        xf = _ffn_block(xf, ada, l, 6, *ffn2, ln_g, ln_b, 2, alpha=alpha, rows_per_batch=S)
    return xf.reshape(B, S, D)
```
